```python
import jax, jax.numpy as jnp
from jax import lax
import numpy as np

D_MODEL = 1024
BATCH = 16
SEQ = 4096
DEPTH = 4

CTX_LEN = 256
GRID_W = 64
N_MIXERS = 2
N_CONV_LAYERS = (DEPTH + 1) // 2
N_RET_LAYERS = DEPTH // 2
CONV_WIDTH = 31
RET_HEADS = 4
RET_DK = D_MODEL // RET_HEADS
RET_DV = D_MODEL // RET_HEADS
RET_QK = RET_HEADS * RET_DK
RET_V = RET_HEADS * RET_DV
RET_IN = 2 * RET_QK + 3 * RET_V
RET_CHUNK = 128
FFN_DIM = ((8 * D_MODEL // 3 + 127) // 128) * 128
FFN_CONV_WIDTH = 3
ROPE_BASE = 10000.0
NORM_EPS = 1e-6
GN_EPS = 1e-5

kernel_name = "hybrid_conformer_retention_dit"


def rmsnorm(x, g):
    x32 = x.astype(jnp.float32)
    y = x32 * lax.rsqrt(jnp.mean(x32 * x32, axis=-1, keepdims=True) + NORM_EPS)
    return (y * g.astype(jnp.float32)).astype(x.dtype)


def layernorm(x, g, b):
    x32 = x.astype(jnp.float32)
    mu = jnp.mean(x32, axis=-1, keepdims=True)
    var = jnp.mean(jnp.square(x32 - mu), axis=-1, keepdims=True)
    y = (x32 - mu) * lax.rsqrt(var + NORM_EPS) * g.astype(jnp.float32) + b.astype(jnp.float32)
    return y.astype(x.dtype)


def modulate(h, shift, scale):
    return h * (1 + scale) + shift


def dwconv(x, w, b):
    k = w.shape[0]
    y = lax.conv_general_dilated(x, w[:, None, :], window_strides=(1,), padding=[(k // 2, k // 2)],
                                 dimension_numbers=("NWC", "WIO", "NWC"),
                                 feature_group_count=x.shape[-1])
    return y + b


def conv_module(h, pw1_w, pw1_b, dw_w, dw_b, ln_g, ln_b, pw2_w, pw2_b):
    z = h @ pw1_w + pw1_b
    a, g = jnp.split(z, 2, axis=-1)
    u = a * jax.nn.sigmoid(g)
    u = dwconv(u, dw_w, dw_b)
    u = jax.nn.silu(layernorm(u, ln_g, ln_b))
    return u @ pw2_w + pw2_b


def conv_ffn(h, w_in, dw_w, dw_b, w_out):
    z = h @ w_in
    u, g = z[..., :FFN_DIM], z[..., FFN_DIM:]
    g = dwconv(g, dw_w, dw_b)
    return (u * jax.nn.silu(g)) @ w_out


def rope_1d(t, pos):
    half = t.shape[-1] // 2
    freqs = ROPE_BASE ** (-jnp.arange(half, dtype=jnp.float32) / half)
    ang = pos[:, None] * freqs[None, :]
    cos = jnp.cos(ang)[None, :, None, :]
    sin = jnp.sin(ang)[None, :, None, :]
    t1, t2 = t[..., :half], t[..., half:]
    return jnp.concatenate([t1 * cos - t2 * sin, t1 * sin + t2 * cos], axis=-1)


def rope_2d(t, row_pos, col_pos):
    half = t.shape[-1] // 2
    t32 = t.astype(jnp.float32)
    out = jnp.concatenate([rope_1d(t32[..., :half], row_pos), rope_1d(t32[..., half:], col_pos)], axis=-1)
    return out.astype(t.dtype)


def chunk_retention(q, k, v, log_gamma, s0):
    b, l, h, _ = q.shape
    n_chunks = l // RET_CHUNK

    def to_chunks(t):
        return t.astype(jnp.float32).reshape(b, n_chunks, RET_CHUNK, h, -1).transpose(1, 0, 3, 2, 4)

    idx = jnp.arange(RET_CHUNK, dtype=jnp.float32)
    diff = idx[:, None] - idx[None, :]
    dmask = jnp.where(diff[None] >= 0, jnp.exp(jnp.maximum(diff, 0.0)[None] * log_gamma[:, None, None]), 0.0)
    xi = jnp.exp((idx + 1.0)[None, :] * log_gamma[:, None])[:, :, None]
    zeta = jnp.exp((RET_CHUNK - 1.0 - idx)[None, :] * log_gamma[:, None])[:, :, None]
    g_chunk = jnp.exp(RET_CHUNK * log_gamma)[:, None, None]

    def step(s, qkv):
        qc, kc, vc = qkv
        scores = jnp.einsum("bhid,bhjd->bhij", qc, kc) * dmask
        o = jnp.einsum("bhij,bhjv->bhiv", scores, vc) + jnp.einsum("bhid,bhdv->bhiv", qc, s) * xi
        s = g_chunk * s + jnp.einsum("bhjd,bhjv->bhdv", kc * zeta, vc)
        return s, o

    s_final, o = lax.scan(step, s0, (to_chunks(q), to_chunks(k), to_chunks(v)))
    o = o.transpose(1, 0, 3, 2, 4).reshape(b, l, h, -1)
    return o, s_final


def retention_project(h, w_in):
    b, l, _ = h.shape
    z = h @ w_in
    q = z[..., :RET_QK].reshape(b, l, RET_HEADS, RET_DK)
    k = z[..., RET_QK:2 * RET_QK].reshape(b, l, RET_HEADS, RET_DK) * (RET_DK ** -0.5)
    v = z[..., 2 * RET_QK:2 * RET_QK + RET_V].reshape(b, l, RET_HEADS, RET_DV)
    g_f = z[..., 2 * RET_QK + RET_V:2 * RET_QK + 2 * RET_V]
    g_b = z[..., 2 * RET_QK + 2 * RET_V:]
    return q, k, v, g_f, g_b


def head_groupnorm(o, g):
    b, l, _, _ = o.shape
    mu = jnp.mean(o, axis=-1, keepdims=True)
    var = jnp.mean(jnp.square(o - mu), axis=-1, keepdims=True)
    y = (o - mu) * lax.rsqrt(var + GN_EPS)
    return y.reshape(b, l, RET_V) * g.astype(jnp.float32)


def retention_output(o_f, o_b, g_f, g_b, gn_f, gn_b, w_out):
    y = (jax.nn.silu(g_f.astype(jnp.float32)) * head_groupnorm(o_f, gn_f)
         + jax.nn.silu(g_b.astype(jnp.float32)) * head_groupnorm(o_b, gn_b))
    return y.astype(w_out.dtype) @ w_out


def retention_mixer(h, hc, w_in, w_out, a_f, a_b, gn_f, gn_b, row_pos, col_pos, with_ctx_out):
    lg_f = -jnp.exp(a_f.astype(jnp.float32))
    lg_b = -jnp.exp(a_b.astype(jnp.float32))
    qc, kc, vc, gcf, gcb = retention_project(hc, w_in)
    s0 = jnp.zeros((hc.shape[0], RET_HEADS, RET_DK, RET_DV), jnp.float32)
    oc_f, sc_f = chunk_retention(qc, kc, vc, lg_f, s0)
    oc_b, sc_b = chunk_retention(qc[:, ::-1], kc[:, ::-1], vc[:, ::-1], lg_b, s0)
    q, k, v, gf, gb = retention_project(h, w_in)
    q = rope_2d(q, row_pos, col_pos)
    k = rope_2d(k, row_pos, col_pos)
    o_f, _ = chunk_retention(q, k, v, lg_f, sc_f)
    o_b, _ = chunk_retention(q[:, ::-1], k[:, ::-1], v[:, ::-1], lg_b, sc_b)
    y = retention_output(o_f, o_b[:, ::-1], gf, gb, gn_f, gn_b, w_out)
    yc = retention_output(oc_f, oc_b[:, ::-1], gcf, gcb, gn_f, gn_b, w_out) if with_ctx_out else None
    return y, yc


def setup_inputs(seed: int = 0) -> dict:
    key = jax.random.key(seed)
    ks = jax.random.split(key, 32)
    f32 = jnp.float32
    d, f = D_MODEL, FFN_DIM
    na, nb = N_CONV_LAYERS, N_RET_LAYERS

    def nrm(k, shape, scale):
        return jax.random.normal(k, shape, f32) * scale

    base_decay = jnp.asarray(np.log(-np.log(1.0 - 2.0 ** (-5.0 - np.arange(RET_HEADS)))), dtype=f32)
    return {
        "x": nrm(ks[0], (BATCH, SEQ, d), 1.0),
        "c": nrm(ks[1], (BATCH, d), 1.0),
        "ctx": nrm(ks[2], (BATCH, CTX_LEN, d), 1.0),
        "c_ctx": nrm(ks[3], (d,), 1.0),
        "ada_w": nrm(ks[4], (DEPTH, d, 6 * d), 0.5 * d ** -0.5),
        "ada_b": nrm(ks[5], (DEPTH, 6 * d), 0.01),
        "norm1_g": 1.0 + nrm(ks[6], (DEPTH, d), 0.02),
        "norm2_g": 1.0 + nrm(ks[7], (DEPTH, d), 0.02),
        "conv_pw1_w": nrm(ks[8], (na, d, 2 * d), d ** -0.5),
        "conv_pw1_b": nrm(ks[9], (na, 2 * d), 0.01),
        "conv_dw_w": nrm(ks[10], (na, CONV_WIDTH, d), CONV_WIDTH ** -0.5),
        "conv_dw_b": nrm(ks[11], (na, d), 0.01),
        "conv_ln_g": 1.0 + nrm(ks[12], (na, d), 0.02),
        "conv_ln_b": nrm(ks[13], (na, d), 0.01),
        "conv_pw2_w": nrm(ks[14], (na, d, d), d ** -0.5),
        "conv_pw2_b": nrm(ks[15], (na, d), 0.01),
        "ret_w_in": nrm(ks[16], (nb, d, RET_IN), d ** -0.5),
        "ret_w_out": nrm(ks[17], (nb, RET_V, d), RET_V ** -0.5),
        "ret_decay_f": base_decay[None, :] + nrm(ks[18], (nb, RET_HEADS), 0.01),
        "ret_decay_b": base_decay[None, :] + nrm(ks[19], (nb, RET_HEADS), 0.01),
        "ret_gn_f": 1.0 + nrm(ks[20], (nb, RET_V), 0.02),
        "ret_gn_b": 1.0 + nrm(ks[21], (nb, RET_V), 0.02),
        "ffn_w_in": nrm(ks[22], (DEPTH, d, 2 * f), d ** -0.5),
        "ffn_dw_w": nrm(ks[23], (DEPTH, FFN_CONV_WIDTH, f), FFN_CONV_WIDTH ** -0.5),
        "ffn_dw_b": nrm(ks[24], (DEPTH, f), 0.01),
        "ffn_w_out": nrm(ks[25], (DEPTH, f, d), f ** -0.5),
        "final_g": 1.0 + nrm(ks[26], (d,), 0.02),
    }


def reference(x, c, ctx, c_ctx, ada_w, ada_b, norm1_g, norm2_g,
              conv_pw1_w, conv_pw1_b, conv_dw_w, conv_dw_b, conv_ln_g, conv_ln_b, conv_pw2_w, conv_pw2_b,
              ret_w_in, ret_w_out, ret_decay_f, ret_decay_b, ret_gn_f, ret_gn_b,
              ffn_w_in, ffn_dw_w, ffn_dw_b, ffn_w_out, final_g):
    seq_len = x.shape[1]
    rows = seq_len // GRID_W
    row_pos = jnp.repeat(jnp.arange(rows, dtype=jnp.float32), GRID_W)
    col_pos = jnp.tile(jnp.arange(GRID_W, dtype=jnp.float32), rows)
    silu_c = jax.nn.silu(c)
    silu_cc = jax.nn.silu(c_ctx)
    hctx = ctx
    for i in range(DEPTH):
        last = i == DEPTH - 1
        j = i // N_MIXERS
        mod_x = (silu_c @ ada_w[i] + ada_b[i])[:, None, :]
        mod_c = (silu_cc @ ada_w[i] + ada_b[i])[None, None, :]
        sx1, cx1, gx1, sx2, cx2, gx2 = jnp.split(mod_x, 6, axis=-1)
        sc1, cc1, gc1, sc2, cc2, gc2 = jnp.split(mod_c, 6, axis=-1)
        hx = modulate(rmsnorm(x, norm1_g[i]), sx1, cx1)
        hc = modulate(rmsnorm(hctx, norm1_g[i]), sc1, cc1)
        if i % N_MIXERS == 0:
            conv_args = (conv_pw1_w[j], conv_pw1_b[j], conv_dw_w[j], conv_dw_b[j],
                         conv_ln_g[j], conv_ln_b[j], conv_pw2_w[j], conv_pw2_b[j])
            yx = conv_module(hx, *conv_args)
            yc = None if last else conv_module(hc, *conv_args)
        else:
            yx, yc = retention_mixer(hx, hc, ret_w_in[j], ret_w_out[j], ret_decay_f[j], ret_decay_b[j],
                                     ret_gn_f[j], ret_gn_b[j], row_pos, col_pos, not last)
        x = x + gx1 * yx
        hx = modulate(rmsnorm(x, norm2_g[i]), sx2, cx2)
        x = x + gx2 * conv_ffn(hx, ffn_w_in[i], ffn_dw_w[i], ffn_dw_b[i], ffn_w_out[i])
        if not last:
            hctx = hctx + gc1 * yc
            hc = modulate(rmsnorm(hctx, norm2_g[i]), sc2, cc2)
            hctx = hctx + gc2 * conv_ffn(hc, ffn_w_in[i], ffn_dw_w[i], ffn_dw_b[i], ffn_w_out[i])
    return rmsnorm(x, final_g)
```

```python
import functools

import jax
import jax.numpy as jnp
from jax import lax
from jax.experimental import pallas as pl
from jax.experimental.pallas import tpu as pltpu

F32 = jnp.float32
BF16 = jnp.bfloat16

NORM_EPS = 1e-6
GN_EPS = 1e-5
ROPE_BASE = 10000.0
GRID_WIDTH = 64
N_HEADS = 4
CHUNK = 128
LANES = 128
HALO = 16
SEQ_TILE = 512
CONV_ROWS = 64
VMEM_LIMIT = 56 * 1024 * 1024


def _sigmoid(x):
    return 1.0 / (1.0 + jnp.exp(-x))


def _silu(x):
    return x * _sigmoid(x)


def _dot(a, b):
    return jnp.dot(a, b, preferred_element_type=F32)


def _rms(x):
    return x * lax.rsqrt(jnp.mean(x * x, axis=-1, keepdims=True) + NORM_EPS)


def _rms_mod(x, g, shift, scale):
    return (_rms(x) * g) * (1.0 + scale) + shift


def _resident(shape):
    nd = len(shape)
    return pl.BlockSpec(shape, lambda *_: (0,) * nd, pipeline_mode=pl.Buffered(1))


def _seq_specs(t, l, d):
    per = t // HALO
    nh = l // HALO
    return [
        pl.BlockSpec((1, t, d), lambda b, i: (b, i, 0)),
        pl.BlockSpec((1, HALO, d), lambda b, i: (b, jnp.maximum(i * per - 1, 0), 0)),
        pl.BlockSpec((1, HALO, d), lambda b, i: (b, jnp.minimum((i + 1) * per, nh - 1), 0)),
    ]


def _params(n_axes=2):
    return pltpu.CompilerParams(dimension_semantics=("parallel",) * n_axes, vmem_limit_bytes=VMEM_LIMIT)


def _mod_kernel(c_ref, w_ref, b_ref, o_ref):
    s = _silu(c_ref[...]).astype(BF16)
    o_ref[0] = _dot(s, w_ref[0].astype(BF16)) + b_ref[0]


def _modulation(cvec, ada_w, ada_b):
    nl, d, d6 = ada_w.shape
    r = cvec.shape[0]
    return pl.pallas_call(
        _mod_kernel,
        grid=(nl, d6 // d),
        in_specs=[
            pl.BlockSpec((r, d), lambda l, j: (0, 0)),
            pl.BlockSpec((1, d, d), lambda l, j: (l, 0, j)),
            pl.BlockSpec((1, 1, d), lambda l, j: (l, 0, j)),
        ],
        out_specs=pl.BlockSpec((1, r, d), lambda l, j: (l, 0, j)),
        out_shape=jax.ShapeDtypeStruct((nl, r, d6), F32),
        compiler_params=_params(),
        name="adaln_modulation",
    )(cvec, ada_w, ada_b.reshape(nl, 1, d6))


def _conv_kernel(xm_ref, xp_ref, xn_ref, mod_ref, ng_ref, pw1_ref, pw1b_ref, dww_ref, dwb_ref,
                 lng_ref, lnb_ref, pw2_ref, pw2b_ref, o_ref, h_scr, u_scr, v_scr, *, t, d, cc, kw):
    i = pl.program_id(1)
    last = pl.num_programs(1) - 1
    mod = mod_ref[0]
    shift, scale, gate = mod[:, 0:d], mod[:, d:2 * d], mod[:, 2 * d:3 * d]
    g = ng_ref[...]
    te = t + 2 * HALO

    h_scr[0:HALO, :] = _rms_mod(xp_ref[0], g, shift, scale).astype(BF16)
    h_scr[HALO:HALO + t, :] = _rms_mod(xm_ref[0], g, shift, scale).astype(BF16)
    h_scr[HALO + t:te, :] = _rms_mod(xn_ref[0], g, shift, scale).astype(BF16)

    rows = lax.broadcasted_iota(jnp.int32, (te, 1), 0)
    inside = jnp.logical_and(jnp.logical_or(rows >= HALO, i > 0),
                             jnp.logical_or(rows < HALO + t, i < last))

    off = HALO - kw // 2
    for c in range(d // cc):
        cs = slice(c * cc, (c + 1) * cc)
        gs = slice(d + c * cc, d + (c + 1) * cc)
        h = h_scr[...]
        a = _dot(h, pw1_ref[:, cs]) + pw1b_ref[:, cs]
        gl = _dot(h, pw1_ref[:, gs]) + pw1b_ref[:, gs]
        u = jnp.where(inside, a * _sigmoid(gl), 0.0)
        u_scr[0] = u
        for s in range(1, 8):
            u_scr[s, 0:te - 8, :] = u_scr[0, pl.ds(s, te - 8), :]

        def row_block(r, carry, c=c):
            r0 = pl.multiple_of(r * CONV_ROWS, CONV_ROWS)
            for lb in range(cc // LANES):
                ls = slice(lb * LANES, (lb + 1) * LANES)
                cols = slice(c * cc + lb * LANES, c * cc + (lb + 1) * LANES)
                acc = jnp.zeros((CONV_ROWS, LANES), F32) + dwb_ref[:, cols]
                for k in range(kw):
                    sh = off + k
                    tap = u_scr[sh % 8, pl.ds(r0 + (sh // 8) * 8, CONV_ROWS), ls]
                    acc = acc + tap * dww_ref[k:k + 1, cols]
                v_scr[pl.ds(r0, CONV_ROWS), cols] = acc
            return carry

        lax.fori_loop(0, t // CONV_ROWS, row_block, 0)

    v = v_scr[...]
    mu = jnp.mean(v, axis=-1, keepdims=True)
    vc = v - mu
    var = jnp.mean(vc * vc, axis=-1, keepdims=True)
    y = vc * lax.rsqrt(var + NORM_EPS) * lng_ref[...] + lnb_ref[...]
    out = _dot(_silu(y).astype(BF16), pw2_ref[...]) + pw2b_ref[...]
    o_ref[0] = xm_ref[0] + gate * out


def _conv_mixer(x, mod, ng, pw1, pw1b, dww, dwb, lng, lnb, pw2, pw2b):
    b, l, d = x.shape
    t = min(SEQ_TILE, l)
    kw = dww.shape[0]
    cc = 256
    assert l % t == 0 and t % CONV_ROWS == 0 and kw // 2 < HALO and d % cc == 0
    kwp = -(-kw // 8) * 8
    dww = jnp.pad(dww, ((0, kwp - kw), (0, 0)))
    te = t + 2 * HALO
    kern = functools.partial(_conv_kernel, t=t, d=d, cc=cc, kw=kw)
    return pl.pallas_call(
        kern,
        grid=(b, l // t),
        in_specs=_seq_specs(t, l, d) + [
            pl.BlockSpec((1, 1, 6 * d), lambda bi, i: (bi, 0, 0)),
            _resident((1, d)),
            _resident((d, 2 * d)), _resident((1, 2 * d)),
            _resident((kwp, d)), _resident((1, d)),
            _resident((1, d)), _resident((1, d)),
            _resident((d, d)), _resident((1, d)),
        ],
        out_specs=pl.BlockSpec((1, t, d), lambda bi, i: (bi, i, 0)),
        out_shape=jax.ShapeDtypeStruct((b, l, d), F32),
        scratch_shapes=[
            pltpu.VMEM((te, d), BF16),
            pltpu.VMEM((8, te, cc), F32),
            pltpu.VMEM((t, d), F32),
        ],
        compiler_params=_params(),
        name="conv_mixer",
    )(x, x, x, mod, ng.reshape(1, d), pw1, pw1b.reshape(1, 2 * d), dww, dwb.reshape(1, d),
      lng.reshape(1, d), lnb.reshape(1, d), pw2, pw2b.reshape(1, d))


def _ffn_kernel(*refs, t, d, f, fc, pre_proj, final_norm):
    refs = list(refs)
    xm_ref, xp_ref, xn_ref = refs[:3]
    pos = 3
    if pre_proj:
        ym_ref, yp_ref, yn_ref, wo_ref = refs[pos:pos + 4]
        pos += 4
    mod_ref, ng_ref, win_ref, dww_ref, dwb_ref, wout_ref = refs[pos:pos + 6]
    pos += 6
    if final_norm:
        fg_ref = refs[pos]
        pos += 1
    o_ref = refs[pos]
    scr = refs[pos + 1:]
    h_scr, g_scr, a_scr = scr[:3]

    i = pl.program_id(1)
    last = pl.num_programs(1) - 1
    mod = mod_ref[0]
    shift, scale, gate2 = mod[:, 3 * d:4 * d], mod[:, 4 * d:5 * d], mod[:, 5 * d:6 * d]
    g = ng_ref[...]
    te = t + 2 * HALO

    xp, xm, xn = xp_ref[0], xm_ref[0], xn_ref[0]
    if pre_proj:
        y_scr = scr[3]
        gate1 = mod[:, 2 * d:3 * d]
        y_scr[0:HALO, :] = yp_ref[0]
        y_scr[HALO:HALO + t, :] = ym_ref[0]
        y_scr[HALO + t:te, :] = yn_ref[0]
        proj = _dot(y_scr[...], wo_ref[...])
        xp = xp + gate1 * proj[0:HALO]
        xm = xm + gate1 * proj[HALO:HALO + t]
        xn = xn + gate1 * proj[HALO + t:te]
    o_ref[0] = xm

    hp = _rms_mod(xp, g, shift, scale)
    hn = _rms_mod(xn, g, shift, scale)
    h_scr[0:HALO, :] = jnp.where(i > 0, hp, 0.0).astype(BF16)
    h_scr[HALO:HALO + t, :] = _rms_mod(xm, g, shift, scale).astype(BF16)
    h_scr[HALO + t:te, :] = jnp.where(i < last, hn, 0.0).astype(BF16)

    for c in range(f // fc):
        us = slice(c * fc, (c + 1) * fc)
        gs = slice(f + c * fc, f + (c + 1) * fc)
        g_scr[...] = _dot(h_scr[...], win_ref[:, gs])
        gc = (g_scr[pl.ds(HALO - 1, t), :] * dww_ref[0:1, us]
              + g_scr[pl.ds(HALO, t), :] * dww_ref[1:2, us]
              + g_scr[pl.ds(HALO + 1, t), :] * dww_ref[2:3, us]
              + dwb_ref[:, us])
        u = _dot(h_scr[pl.ds(HALO, t), :], win_ref[:, us])
        a_scr[:, us] = (u * _silu(gc)).astype(BF16)

    res = o_ref[0] + gate2 * _dot(a_scr[...], wout_ref[...])
    if final_norm:
        res = _rms(res) * fg_ref[...]
    o_ref[0] = res


def _conv_ffn(x, mod, ng, win, dww, dwb, wout, pre=None, final_g=None):
    b, l, d = x.shape
    f = wout.shape[0]
    t = min(SEQ_TILE, l)
    fc = f // 2 if (f // 2) % LANES == 0 else f
    assert l % t == 0 and t % HALO == 0 and dww.shape[0] == 3
    te = t + 2 * HALO
    seq = _seq_specs(t, l, d)
    in_specs = list(seq)
    args = [x, x, x]
    scratch = [pltpu.VMEM((te, d), BF16), pltpu.VMEM((te, fc), F32), pltpu.VMEM((t, f), BF16)]
    if pre is not None:
        y, wo = pre
        in_specs += list(seq) + [_resident(wo.shape)]
        args += [y, y, y, wo]
        scratch.append(pltpu.VMEM((te, d), BF16))
    in_specs += [
        pl.BlockSpec((1, 1, 6 * d), lambda bi, i: (bi, 0, 0)),
        _resident((1, d)),
        _resident((d, 2 * f)),
        _resident((8, f)), _resident((1, f)),
        _resident((f, d)),
    ]
    args += [mod, ng.reshape(1, d), win, jnp.pad(dww, ((0, 5), (0, 0))), dwb.reshape(1, f), wout]
    if final_g is not None:
        in_specs.append(_resident((1, d)))
        args.append(final_g.reshape(1, d))
    kern = functools.partial(_ffn_kernel, t=t, d=d, f=f, fc=fc, pre_proj=pre is not None,
                             final_norm=final_g is not None)
    return pl.pallas_call(
        kern,
        grid=(b, l // t),
        in_specs=in_specs,
        out_specs=pl.BlockSpec((1, t, d), lambda bi, i: (bi, i, 0)),
        out_shape=jax.ShapeDtypeStruct((b, l, d), F32),
        scratch_shapes=scratch,
        compiler_params=_params(),
        name="conv_ffn",
    )(*args)


def _proj_kernel(*refs, d, rope):
    if rope:
        x_ref, mod_ref, ng_ref, w_ref, cos_ref, sin_ref, o_ref = refs
    else:
        x_ref, mod_ref, ng_ref, w_ref, o_ref = refs
    mod = mod_ref[0]
    shift, scale = mod[:, 0:d], mod[:, d:2 * d]
    h = _rms_mod(x_ref[0], ng_ref[...], shift, scale).astype(BF16)
    dk = d // N_HEADS
    kscale = dk ** -0.5
    for part in range(5):
        z = _dot(h, w_ref[:, part * d:(part + 1) * d])
        if part == 1:
            z = z * kscale
        if rope and part < 2:
            for g in range(d // LANES):
                tab = slice((g % (dk // LANES)) * LANES, (g % (dk // LANES) + 1) * LANES)
                zg = z[:, g * LANES:(g + 1) * LANES]
                sw = pltpu.roll(zg, LANES // 2, 1)
                o_ref[0, :, part * d + g * LANES:part * d + (g + 1) * LANES] = (
                    zg * cos_ref[:, tab] + sw * sin_ref[:, tab]).astype(BF16)
        else:
            o_ref[0, :, part * d:(part + 1) * d] = z.astype(BF16)


def _ret_project(x, mod, ng, w_in, tables=None):
    b, l, d = x.shape
    t = min(SEQ_TILE, l)
    assert l % t == 0 and w_in.shape[1] == 5 * d and d // N_HEADS == 2 * LANES
    dk = d // N_HEADS
    in_specs = [
        pl.BlockSpec((1, t, d), lambda bi, i: (bi, i, 0)),
        pl.BlockSpec((1, 1, 6 * d), lambda bi, i: (bi, 0, 0)),
        _resident((1, d)),
        _resident((d, 5 * d)),
    ]
    args = [x, mod, ng.reshape(1, d), w_in]
    if tables is not None:
        in_specs += [pl.BlockSpec((t, dk), lambda bi, i: (i, 0))] * 2
        args += list(tables)
    return pl.pallas_call(
        functools.partial(_proj_kernel, d=d, rope=tables is not None),
        grid=(b, l // t),
        in_specs=in_specs,
        out_specs=pl.BlockSpec((1, t, 5 * d), lambda bi, i: (bi, i, 0)),
        out_shape=jax.ShapeDtypeStruct((b, l, 5 * d), BF16),
        compiler_params=_params(),
        name="ret_project",
    )(*args)


def _rope_tables(l, dk):
    half = dk // 4
    pos = jnp.arange(l, dtype=jnp.int32)
    row = (pos // GRID_WIDTH).astype(F32)
    col = (pos % GRID_WIDTH).astype(F32)
    freqs = ROPE_BASE ** (-jnp.arange(half, dtype=F32) / half)
    cos, sin = [], []
    for p in (row, col):
        ang = p[:, None] * freqs[None, :]
        cos += [jnp.cos(ang), jnp.cos(ang)]
        sin += [-jnp.sin(ang), jnp.sin(ang)]
    return jnp.concatenate(cos, axis=-1), jnp.concatenate(sin, axis=-1)


def _ret_kernel(q_ref, k_ref, v_ref, gf_ref, gb_ref, qc_ref, kc_ref, vc_ref, gfc_ref, gbc_ref,
                af_ref, ab_ref, gnf_ref, gnb_ref, y_ref, yc_ref, of_scr, ob_scr, sf_scr, sb_scr,
                *, n_lat, n_ctx):
    c = CHUNK
    dv = sf_scr.shape[1]
    lgf = -jnp.exp(af_ref[0])
    lgb = -jnp.exp(ab_ref[0])
    ri = lax.broadcasted_iota(jnp.int32, (c, c), 0)
    ci = lax.broadcasted_iota(jnp.int32, (c, c), 1)
    diff = (ri - ci).astype(F32)
    d_f = jnp.where(diff >= 0, jnp.exp(jnp.maximum(diff, 0.0) * lgf[:, :c]), 0.0)
    d_b = jnp.where(diff <= 0, jnp.exp(jnp.maximum(-diff, 0.0) * lgb[:, :c]), 0.0)
    idx = lax.broadcasted_iota(jnp.int32, (c, dv), 0).astype(F32)
    xi_f = jnp.exp((idx + 1.0) * lgf)
    zeta_f = jnp.exp((c - 1.0 - idx) * lgf)
    xi_b = jnp.exp((c - idx) * lgb)
    zeta_b = jnp.exp(idx * lgb)
    gc_f = jnp.exp(c * lgf)
    gc_b = jnp.exp(c * lgb)
    gnf = gnf_ref[...]
    gnb = gnb_ref[...]
    tn = (((0,), (0,)), ((), ()))
    nt = (((1,), (1,)), ((), ()))

    def group_norm(o):
        mu = jnp.mean(o, axis=-1, keepdims=True)
        oc = o - mu
        var = jnp.mean(oc * oc, axis=-1, keepdims=True)
        return oc * lax.rsqrt(var + GN_EPS)

    def scan(q_r, k_r, v_r, gf_r, gb_r, out_r, n_chunks):
        def fwd(n, carry):
            rows = pl.ds(pl.multiple_of(n * c, c), c)
            q, k, v = q_r[0, rows, :], k_r[0, rows, :], v_r[0, rows, :]
            a = lax.dot_general(q, k, nt, preferred_element_type=F32)
            s = sf_scr[...]
            of_scr[rows, :] = _dot((a * d_f).astype(BF16), v) + _dot(q, s.astype(BF16)) * xi_f
            ob_scr[rows, :] = _dot((a * d_b).astype(BF16), v)
            kz = (k.astype(F32) * zeta_f).astype(BF16)
            sf_scr[...] = gc_f * s + lax.dot_general(kz, v, tn, preferred_element_type=F32)
            return carry

        lax.fori_loop(0, n_chunks, fwd, 0)

        def bwd(m, carry):
            rows = pl.ds(pl.multiple_of((n_chunks - 1 - m) * c, c), c)
            q, k, v = q_r[0, rows, :], k_r[0, rows, :], v_r[0, rows, :]
            s = sb_scr[...]
            ob = ob_scr[rows, :] + _dot(q, s.astype(BF16)) * xi_b
            kz = (k.astype(F32) * zeta_b).astype(BF16)
            sb_scr[...] = gc_b * s + lax.dot_general(kz, v, tn, preferred_element_type=F32)
            y = (_silu(gf_r[0, rows, :].astype(F32)) * (group_norm(of_scr[rows, :]) * gnf)
                 + _silu(gb_r[0, rows, :].astype(F32)) * (group_norm(ob) * gnb))
            out_r[0, rows, :] = y.astype(BF16)
            return carry

        lax.fori_loop(0, n_chunks, bwd, 0)

    sf_scr[...] = jnp.zeros_like(sf_scr)
    sb_scr[...] = jnp.zeros_like(sb_scr)
    scan(qc_ref, kc_ref, vc_ref, gfc_ref, gbc_ref, yc_ref, n_ctx)
    scan(q_ref, k_ref, v_ref, gf_ref, gb_ref, y_ref, n_lat)


def _retention(z, zc, a_f, a_b, gn_f, gn_b):
    b, l, d5 = z.shape
    lc = zc.shape[1]
    d = d5 // 5
    dk = d // N_HEADS
    assert l % CHUNK == 0 and lc % CHUNK == 0 and lc <= l

    def part_specs(n_rows):
        return [pl.BlockSpec((1, n_rows, dk), lambda bi, h, p=p: (bi, 0, p * N_HEADS + h)) for p in range(5)]

    head_vec = pl.BlockSpec((1, 1, dk), lambda bi, h: (h, 0, 0))
    gn_spec = pl.BlockSpec((1, dk), lambda bi, h: (0, h))
    kern = functools.partial(_ret_kernel, n_lat=l // CHUNK, n_ctx=lc // CHUNK)
    spread = lambda a: jnp.broadcast_to(a.astype(F32)[:, None, None], (N_HEADS, 1, dk))
    return pl.pallas_call(
        kern,
        grid=(b, N_HEADS),
        in_specs=part_specs(l) + part_specs(lc) + [head_vec, head_vec, gn_spec, gn_spec],
        out_specs=[pl.BlockSpec((1, l, dk), lambda bi, h: (bi, 0, h)),
                   pl.BlockSpec((1, lc, dk), lambda bi, h: (bi, 0, h))],
        out_shape=[jax.ShapeDtypeStruct((b, l, d), BF16), jax.ShapeDtypeStruct((b, lc, d), BF16)],
        scratch_shapes=[pltpu.VMEM((l, dk), F32), pltpu.VMEM((l, dk), F32),
                        pltpu.VMEM((dk, dk), F32), pltpu.VMEM((dk, dk), F32)],
        compiler_params=_params(),
        name="retention",
    )(z, z, z, z, z, zc, zc, zc, zc, zc, spread(a_f), spread(a_b),
      gn_f.reshape(1, d).astype(F32), gn_b.reshape(1, d).astype(F32))


def kernel(x, c, ctx, c_ctx, ada_w, ada_b, norm1_g, norm2_g, conv_pw1_w, conv_pw1_b, conv_dw_w, conv_dw_b, conv_ln_g, conv_ln_b, conv_pw2_w, conv_pw2_b, ret_w_in, ret_w_out, ret_decay_f, ret_decay_b, ret_gn_f, ret_gn_b, ffn_w_in, ffn_dw_w, ffn_dw_b, ffn_w_out, final_g):
    b, l, d = x.shape
    depth = ada_w.shape[0]
    n_mixers = 2

    rows = -(-(b + 1) // 8) * 8
    cvec = jnp.concatenate([c, c_ctx[None, :], jnp.zeros((rows - b - 1, d), F32)], axis=0)
    mod = _modulation(cvec, ada_w, ada_b)
    tables = _rope_tables(l, d // N_HEADS)

    hctx = ctx
    for i in range(depth):
        last = i == depth - 1
        j = i // n_mixers
        mod_x = mod[i, :b][:, None, :]
        mod_c = jnp.broadcast_to(mod[i, b][None, None, :], (b, 1, 6 * d))
        pre_x = pre_c = None
        if i % n_mixers == 0:
            cargs = (norm1_g[i], conv_pw1_w[j].astype(BF16), conv_pw1_b[j], conv_dw_w[j], conv_dw_b[j],
                     conv_ln_g[j], conv_ln_b[j], conv_pw2_w[j].astype(BF16), conv_pw2_b[j])
            x = _conv_mixer(x, mod_x, *cargs)
            if not last:
                hctx = _conv_mixer(hctx, mod_c, *cargs)
        else:
            w_in = ret_w_in[j].astype(BF16)
            z = _ret_project(x, mod_x, norm1_g[i], w_in, tables)
            zc = _ret_project(hctx, mod_c, norm1_g[i], w_in)
            y, yc = _retention(z, zc, ret_decay_f[j], ret_decay_b[j], ret_gn_f[j], ret_gn_b[j])
            w_out = ret_w_out[j].astype(BF16)
            pre_x, pre_c = (y, w_out), (yc, w_out)
        fargs = (norm2_g[i], ffn_w_in[i].astype(BF16), ffn_dw_w[i], ffn_dw_b[i], ffn_w_out[i].astype(BF16))
        x = _conv_ffn(x, mod_x, *fargs, pre=pre_x, final_g=final_g if last else None)
        if not last:
            hctx = _conv_ffn(hctx, mod_c, *fargs, pre=pre_c)
    return x
```

```python
import functools

import jax
import jax.numpy as jnp
from jax import lax
from jax.experimental import pallas as pl
from jax.experimental.pallas import tpu as pltpu

F32 = jnp.float32
BF16 = jnp.bfloat16

NORM_EPS = 1e-6
GN_EPS = 1e-5
ROPE_BASE = 10000.0
GRID_WIDTH = 64
N_HEADS = 4
CHUNK = 128
MXU_TILE = 256
LANES = 128
HALO = 16
SEQ_TILE = 512
CONV_ROWS = 32
VMEM_LIMIT = 56 * 1024 * 1024


def _sigmoid(x):
    return 1.0 / (1.0 + jnp.exp(-x))


def _silu(x):
    return x * _sigmoid(x)


def _dot(a, b):
    return jnp.dot(a, b, preferred_element_type=F32)


def _rms(x):
    return x * lax.rsqrt(jnp.mean(x * x, axis=-1, keepdims=True) + NORM_EPS)


def _rms_mod(x, g, shift, scale):
    return (_rms(x) * g) * (1.0 + scale) + shift


def _resident(shape):
    nd = len(shape)
    return pl.BlockSpec(shape, lambda *_: (0,) * nd, pipeline_mode=pl.Buffered(1))


def _seq_specs(t, l, d):
    per = t // HALO
    nh = l // HALO
    return [
        pl.BlockSpec((1, t, d), lambda b, i: (b, i, 0)),
        pl.BlockSpec((1, HALO, d), lambda b, i: (b, jnp.maximum(i * per - 1, 0), 0)),
        pl.BlockSpec((1, HALO, d), lambda b, i: (b, jnp.minimum((i + 1) * per, nh - 1), 0)),
    ]


def _params(n_axes=2):
    return pltpu.CompilerParams(dimension_semantics=("parallel",) * n_axes, vmem_limit_bytes=VMEM_LIMIT)


def _mod_kernel(c_ref, w_ref, b_ref, o_ref):
    s = _silu(c_ref[...]).astype(BF16)
    o_ref[0] = _dot(s, w_ref[0].astype(BF16)) + b_ref[0]


def _modulation(cvec, ada_w, ada_b):
    nl, d, d6 = ada_w.shape
    r = cvec.shape[0]
    return pl.pallas_call(
        _mod_kernel,
        grid=(nl, d6 // d),
        in_specs=[
            pl.BlockSpec((r, d), lambda l, j: (0, 0)),
            pl.BlockSpec((1, d, d), lambda l, j: (l, 0, j)),
            pl.BlockSpec((1, 1, d), lambda l, j: (l, 0, j)),
        ],
        out_specs=pl.BlockSpec((1, r, d), lambda l, j: (l, 0, j)),
        out_shape=jax.ShapeDtypeStruct((nl, r, d6), F32),
        compiler_params=_params(),
        name="adaln_modulation",
    )(cvec, ada_w, ada_b.reshape(nl, 1, d6))


def _conv_kernel(xm_ref, xp_ref, xn_ref, mod_ref, ng_ref, w3_ref, b3_ref, dww_ref, dwb_ref,
                 lng_ref, lnb_ref, pw2_ref, pw2b_ref, o_ref, h_scr, u_scr, v_scr, *, t, d, kw):
    i = pl.program_id(1)
    last = pl.num_programs(1) - 1
    mod = mod_ref[0]
    shift, scale, gate = mod[:, 0:d], mod[:, d:2 * d], mod[:, 2 * d:3 * d]
    g = ng_ref[...]
    te = t + 2 * HALO
    nc = d // LANES

    h_scr[0:HALO, :] = _rms_mod(xp_ref[0], g, shift, scale).astype(BF16)
    h_scr[HALO:HALO + t, :] = _rms_mod(xm_ref[0], g, shift, scale).astype(BF16)
    h_scr[HALO + t:te, :] = _rms_mod(xn_ref[0], g, shift, scale).astype(BF16)

    rows = lax.broadcasted_iota(jnp.int32, (te, 1), 0)
    inside = jnp.logical_and(jnp.logical_or(rows >= HALO, i > 0),
                             jnp.logical_or(rows < HALO + t, i < last))
    off = HALO - kw // 2

    def produce(c, slot):
        z = _dot(h_scr[...], w3_ref[c]) + b3_ref[c]
        u = jnp.where(inside, z[:, :LANES] * _sigmoid(z[:, LANES:]), 0.0)
        u_scr[slot, pl.ds(0, te, stride=2), :] = u

    def conv(c, slot):
        for rb in range(t // CONV_ROWS):
            acc = jnp.zeros((CONV_ROWS, LANES), F32) + dwb_ref[c]
            for k in range(kw):
                tap = u_scr[slot, pl.ds(2 * (rb * CONV_ROWS + off + k), CONV_ROWS, stride=2), :]
                acc = acc + tap * dww_ref[c, k:k + 1, :]
            v_scr[c, rb * CONV_ROWS:(rb + 1) * CONV_ROWS, :] = acc

    produce(0, 0)

    def body(c, carry):
        slot = lax.rem(c, 2)
        conv(c, slot)
        produce(c + 1, 1 - slot)
        return carry

    lax.fori_loop(0, nc - 1, body, 0)
    conv(nc - 1, (nc - 1) % 2)

    v = jnp.concatenate([v_scr[c] for c in range(nc)], axis=1)
    mu = jnp.mean(v, axis=-1, keepdims=True)
    vc = v - mu
    var = jnp.mean(vc * vc, axis=-1, keepdims=True)
    y = vc * lax.rsqrt(var + NORM_EPS) * lng_ref[...] + lnb_ref[...]
    out = _dot(_silu(y).astype(BF16), pw2_ref[...]) + pw2b_ref[...]
    o_ref[0] = xm_ref[0] + gate * out


def _conv_mixer(x, mod, ng, pw1, pw1b, dww, dwb, lng, lnb, pw2, pw2b):
    b, l, d = x.shape
    t = min(SEQ_TILE, l)
    kw = dww.shape[0]
    nc = d // LANES
    assert l % t == 0 and t % CONV_ROWS == 0 and kw // 2 < HALO and d % LANES == 0
    kwp = -(-kw // 8) * 8
    te = t + 2 * HALO
    w3 = jnp.concatenate([pw1[:, :d].reshape(d, nc, LANES), pw1[:, d:].reshape(d, nc, LANES)],
                         axis=2).transpose(1, 0, 2)
    b3 = jnp.concatenate([pw1b[:d].reshape(nc, 1, LANES), pw1b[d:].reshape(nc, 1, LANES)], axis=2)
    dww3 = jnp.pad(dww, ((0, kwp - kw), (0, 0))).reshape(kwp, nc, LANES).transpose(1, 0, 2)
    dwb3 = dwb.reshape(nc, 1, LANES)
    kern = functools.partial(_conv_kernel, t=t, d=d, kw=kw)
    return pl.pallas_call(
        kern,
        grid=(b, l // t),
        in_specs=_seq_specs(t, l, d) + [
            pl.BlockSpec((1, 1, 6 * d), lambda bi, i: (bi, 0, 0)),
            _resident((1, d)),
            _resident((nc, d, 2 * LANES)), _resident((nc, 1, 2 * LANES)),
            _resident((nc, kwp, LANES)), _resident((nc, 1, LANES)),
            _resident((1, d)), _resident((1, d)),
            _resident((d, d)), _resident((1, d)),
        ],
        out_specs=pl.BlockSpec((1, t, d), lambda bi, i: (bi, i, 0)),
        out_shape=jax.ShapeDtypeStruct((b, l, d), F32),
        scratch_shapes=[
            pltpu.VMEM((te, d), BF16),
            pltpu.VMEM((2, 2 * te, LANES), F32),
            pltpu.VMEM((nc, t, LANES), F32),
        ],
        compiler_params=_params(),
        name="conv_mixer",
    )(x, x, x, mod, ng.reshape(1, d), w3, b3, dww3, dwb3,
      lng.reshape(1, d), lnb.reshape(1, d), pw2, pw2b.reshape(1, d))


def _ffn_kernel(*refs, t, d, f, chunks, pre_proj, final_norm):
    refs = list(refs)
    xm_ref, xp_ref, xn_ref = refs[:3]
    pos = 3
    if pre_proj:
        ym_ref, yp_ref, yn_ref, wo_ref = refs[pos:pos + 4]
        pos += 4
    mod_ref, ng_ref, win_ref, dww_ref, dwb_ref, wout_ref = refs[pos:pos + 6]
    pos += 6
    if final_norm:
        fg_ref = refs[pos]
        pos += 1
    o_ref = refs[pos]
    scr = refs[pos + 1:]
    h_scr, g_scr, a_scr = scr[:3]

    i = pl.program_id(1)
    last = pl.num_programs(1) - 1
    mod = mod_ref[0]
    shift, scale, gate2 = mod[:, 3 * d:4 * d], mod[:, 4 * d:5 * d], mod[:, 5 * d:6 * d]
    g = ng_ref[...]
    te = t + 2 * HALO

    xp, xm, xn = xp_ref[0], xm_ref[0], xn_ref[0]
    if pre_proj:
        y_scr = scr[3]
        gate1 = mod[:, 2 * d:3 * d]
        y_scr[0:HALO, :] = yp_ref[0]
        y_scr[HALO:HALO + t, :] = ym_ref[0]
        y_scr[HALO + t:te, :] = yn_ref[0]
        proj = _dot(y_scr[...], wo_ref[...])
        xp = xp + gate1 * proj[0:HALO]
        xm = xm + gate1 * proj[HALO:HALO + t]
        xn = xn + gate1 * proj[HALO + t:te]
    o_ref[0] = xm

    hp = _rms_mod(xp, g, shift, scale)
    hn = _rms_mod(xn, g, shift, scale)
    h_scr[0:HALO, :] = jnp.where(i > 0, hp, 0.0).astype(BF16)
    h_scr[HALO:HALO + t, :] = _rms_mod(xm, g, shift, scale).astype(BF16)
    h_scr[HALO + t:te, :] = jnp.where(i < last, hn, 0.0).astype(BF16)

    for lo, width in chunks:
        us = slice(lo, lo + width)
        gs = slice(f + lo, f + lo + width)
        g_scr[:, 0:width] = _dot(h_scr[...], win_ref[:, gs])
        gc = (g_scr[pl.ds(HALO - 1, t), 0:width] * dww_ref[0:1, us]
              + g_scr[pl.ds(HALO, t), 0:width] * dww_ref[1:2, us]
              + g_scr[pl.ds(HALO + 1, t), 0:width] * dww_ref[2:3, us]
              + dwb_ref[:, us])
        u = _dot(h_scr[pl.ds(HALO, t), :], win_ref[:, us])
        a_scr[:, us] = (u * _silu(gc)).astype(BF16)

    res = o_ref[0] + gate2 * _dot(a_scr[...], wout_ref[...])
    if final_norm:
        res = _rms(res) * fg_ref[...]
    o_ref[0] = res


def _conv_ffn(x, mod, ng, win, dww, dwb, wout, pre=None, final_g=None):
    b, l, d = x.shape
    f = wout.shape[0]
    t = min(SEQ_TILE, l)
    tiles = f // MXU_TILE
    first = -(-tiles // 2) * MXU_TILE
    chunks = ((0, first), (first, f - first)) if f % MXU_TILE == 0 and tiles > 1 else ((0, f),)
    fc = chunks[0][1]
    assert l % t == 0 and t % HALO == 0 and dww.shape[0] == 3
    te = t + 2 * HALO
    seq = _seq_specs(t, l, d)
    in_specs = list(seq)
    args = [x, x, x]
    scratch = [pltpu.VMEM((te, d), BF16), pltpu.VMEM((te, fc), F32), pltpu.VMEM((t, f), BF16)]
    if pre is not None:
        y, wo = pre
        in_specs += list(seq) + [_resident(wo.shape)]
        args += [y, y, y, wo]
        scratch.append(pltpu.VMEM((te, d), BF16))
    in_specs += [
        pl.BlockSpec((1, 1, 6 * d), lambda bi, i: (bi, 0, 0)),
        _resident((1, d)),
        _resident((d, 2 * f)),
        _resident((8, f)), _resident((1, f)),
        _resident((f, d)),
    ]
    args += [mod, ng.reshape(1, d), win, jnp.pad(dww, ((0, 5), (0, 0))), dwb.reshape(1, f), wout]
    if final_g is not None:
        in_specs.append(_resident((1, d)))
        args.append(final_g.reshape(1, d))
    kern = functools.partial(_ffn_kernel, t=t, d=d, f=f, chunks=chunks, pre_proj=pre is not None,
                             final_norm=final_g is not None)
    return pl.pallas_call(
        kern,
        grid=(b, l // t),
        in_specs=in_specs,
        out_specs=pl.BlockSpec((1, t, d), lambda bi, i: (bi, i, 0)),
        out_shape=jax.ShapeDtypeStruct((b, l, d), F32),
        scratch_shapes=scratch,
        compiler_params=_params(),
        name="conv_ffn",
    )(*args)


def _proj_kernel(*refs, d, rope):
    if rope:
        x_ref, mod_ref, ng_ref, w_ref, cos_ref, sin_ref, o_ref = refs
    else:
        x_ref, mod_ref, ng_ref, w_ref, o_ref = refs
    mod = mod_ref[0]
    shift, scale = mod[:, 0:d], mod[:, d:2 * d]
    h = _rms_mod(x_ref[0], ng_ref[...], shift, scale).astype(BF16)
    dk = d // N_HEADS
    kscale = dk ** -0.5
    for part in range(5):
        z = _dot(h, w_ref[:, part * d:(part + 1) * d])
        if part == 1:
            z = z * kscale
        if rope and part < 2:
            for g in range(d // LANES):
                tab = slice((g % (dk // LANES)) * LANES, (g % (dk // LANES) + 1) * LANES)
                zg = z[:, g * LANES:(g + 1) * LANES]
                sw = pltpu.roll(zg, LANES // 2, 1)
                o_ref[0, :, part * d + g * LANES:part * d + (g + 1) * LANES] = (
                    zg * cos_ref[:, tab] + sw * sin_ref[:, tab]).astype(BF16)
        else:
            o_ref[0, :, part * d:(part + 1) * d] = z.astype(BF16)


def _ret_project(x, mod, ng, w_in, tables=None):
    b, l, d = x.shape
    t = min(SEQ_TILE, l)
    assert l % t == 0 and w_in.shape[1] == 5 * d and d // N_HEADS == 2 * LANES
    dk = d // N_HEADS
    in_specs = [
        pl.BlockSpec((1, t, d), lambda bi, i: (bi, i, 0)),
        pl.BlockSpec((1, 1, 6 * d), lambda bi, i: (bi, 0, 0)),
        _resident((1, d)),
        _resident((d, 5 * d)),
    ]
    args = [x, mod, ng.reshape(1, d), w_in]
    if tables is not None:
        in_specs += [pl.BlockSpec((t, dk), lambda bi, i: (i, 0))] * 2
        args += list(tables)
    return pl.pallas_call(
        functools.partial(_proj_kernel, d=d, rope=tables is not None),
        grid=(b, l // t),
        in_specs=in_specs,
        out_specs=pl.BlockSpec((1, t, 5 * d), lambda bi, i: (bi, i, 0)),
        out_shape=jax.ShapeDtypeStruct((b, l, 5 * d), BF16),
        compiler_params=_params(),
        name="ret_project",
    )(*args)


def _rope_tables(l, dk):
    half = dk // 4
    pos = jnp.arange(l, dtype=jnp.int32)
    row = (pos // GRID_WIDTH).astype(F32)
    col = (pos % GRID_WIDTH).astype(F32)
    freqs = ROPE_BASE ** (-jnp.arange(half, dtype=F32) / half)
    cos, sin = [], []
    for p in (row, col):
        ang = p[:, None] * freqs[None, :]
        cos += [jnp.cos(ang), jnp.cos(ang)]
        sin += [-jnp.sin(ang), jnp.sin(ang)]
    return jnp.concatenate(cos, axis=-1), jnp.concatenate(sin, axis=-1)


def _ret_kernel(q_ref, k_ref, v_ref, gf_ref, gb_ref, qc_ref, kc_ref, vc_ref, gfc_ref, gbc_ref,
                af_ref, ab_ref, gnf_ref, gnb_ref, y_ref, yc_ref,
                of_scr, ob_scr, y_scr, sf_scr, sb_scr, df_scr, db_scr, tab_scr,
                *, n_lat, n_ctx):
    c = df_scr.shape[0]
    dv = sf_scr.shape[1]
    lgf = -jnp.exp(af_ref[0])
    lgb = -jnp.exp(ab_ref[0])
    ri = lax.broadcasted_iota(jnp.int32, (c, c), 0)
    ci = lax.broadcasted_iota(jnp.int32, (c, c), 1)
    diff = (ri - ci).astype(F32)
    df_scr[...] = jnp.where(diff >= 0, jnp.exp(jnp.maximum(diff, 0.0) * lgf[:, :c]), 0.0)
    db_scr[...] = jnp.where(diff <= 0, jnp.exp(jnp.maximum(-diff, 0.0) * lgb[:, :c]), 0.0)
    idx = lax.broadcasted_iota(jnp.int32, (c, dv), 0).astype(F32)
    tab_scr[0] = jnp.exp((idx + 1.0) * lgf).astype(BF16)
    tab_scr[1] = jnp.exp((c - 1.0 - idx) * lgf).astype(BF16)
    tab_scr[2] = jnp.exp((c - idx) * lgb).astype(BF16)
    tab_scr[3] = jnp.exp(idx * lgb).astype(BF16)
    gc_f = jnp.exp(c * lgf)
    gc_b = jnp.exp(c * lgb)
    gnf = gnf_ref[...]
    gnb = gnb_ref[...]
    tn = (((0,), (0,)), ((), ()))
    nt = (((1,), (1,)), ((), ()))

    def group_norm(o):
        mu = jnp.mean(o, axis=-1, keepdims=True)
        oc = o - mu
        var = jnp.mean(oc * oc, axis=-1, keepdims=True)
        return oc * lax.rsqrt(var + GN_EPS)

    def scan(q_r, k_r, v_r, gf_r, gb_r, out_r, n_chunks):
        def rows_of(n):
            return pl.ds(pl.multiple_of(n * c, c), c)

        def mm_fwd(n):
            rows = rows_of(n)
            q, k, v = q_r[0, rows, :], k_r[0, rows, :], v_r[0, rows, :]
            a = lax.dot_general(q, k, nt, preferred_element_type=F32)
            s = sf_scr[...]
            of_scr[rows, :] = _dot((a * df_scr[...]).astype(BF16), v) + _dot(q * tab_scr[0], s.astype(BF16))
            sf_scr[...] = gc_f * s + lax.dot_general(k * tab_scr[1], v, tn, preferred_element_type=F32)

        def mm_bwd(n):
            rows = rows_of(n)
            q, k, v = q_r[0, rows, :], k_r[0, rows, :], v_r[0, rows, :]
            a = lax.dot_general(q, k, nt, preferred_element_type=F32)
            s = sb_scr[...]
            ob_scr[rows, :] = _dot((a * db_scr[...]).astype(BF16), v) + _dot(q * tab_scr[2], s.astype(BF16))
            sb_scr[...] = gc_b * s + lax.dot_general(k * tab_scr[3], v, tn, preferred_element_type=F32)

        def gate(n, raw_scr, gate_r, gn, first):
            rows = rows_of(n)
            term = _silu(gate_r[0, rows, :].astype(F32)) * (group_norm(raw_scr[rows, :]) * gn)
            if first:
                y_scr[rows, :] = term
            else:
                out_r[0, rows, :] = (y_scr[rows, :] + term).astype(BF16)

        if n_chunks == 1:
            mm_fwd(0)
            mm_bwd(0)
            gate(0, of_scr, gf_r, gnf, True)
            gate(0, ob_scr, gb_r, gnb, False)
            return
        assert n_chunks % 2 == 0
        half = n_chunks // 2
        mm_fwd(0)
        mm_bwd(n_chunks - 1)

        def step(s, first):
            gate(s - 1, of_scr, gf_r, gnf, first)
            gate(n_chunks - s, ob_scr, gb_r, gnb, first)
            mm_fwd(s)
            mm_bwd(n_chunks - 1 - s)

        def first_half(s, carry):
            step(s, True)
            return carry

        def second_half(s, carry):
            step(s, False)
            return carry

        lax.fori_loop(1, half + 1, first_half, 0)
        lax.fori_loop(half + 1, n_chunks, second_half, 0)
        gate(n_chunks - 1, of_scr, gf_r, gnf, False)
        gate(0, ob_scr, gb_r, gnb, False)

    sf_scr[...] = jnp.zeros_like(sf_scr)
    sb_scr[...] = jnp.zeros_like(sb_scr)
    scan(qc_ref, kc_ref, vc_ref, gfc_ref, gbc_ref, yc_ref, n_ctx)
    scan(q_ref, k_ref, v_ref, gf_ref, gb_ref, y_ref, n_lat)


def _ret_chunk(l, lc):
    for c in (MXU_TILE, CHUNK):
        if all(n % c == 0 and (n // c == 1 or (n // c) % 2 == 0) for n in (l, lc)):
            return c
    raise ValueError(f"unsupported sequence lengths {l}, {lc}")


def _retention(z, zc, a_f, a_b, gn_f, gn_b):
    b, l, d5 = z.shape
    lc = zc.shape[1]
    d = d5 // 5
    dk = d // N_HEADS
    c = _ret_chunk(l, lc)
    assert lc <= l

    def part_specs(n_rows):
        return [pl.BlockSpec((1, n_rows, dk), lambda bi, h, p=p: (bi, 0, p * N_HEADS + h)) for p in range(5)]

    head_vec = pl.BlockSpec((1, 1, dk), lambda bi, h: (h, 0, 0))
    gn_spec = pl.BlockSpec((1, dk), lambda bi, h: (0, h))
    kern = functools.partial(_ret_kernel, n_lat=l // c, n_ctx=lc // c)
    spread = lambda a: jnp.broadcast_to(a.astype(F32)[:, None, None], (N_HEADS, 1, dk))
    return pl.pallas_call(
        kern,
        grid=(b, N_HEADS),
        in_specs=part_specs(l) + part_specs(lc) + [head_vec, head_vec, gn_spec, gn_spec],
        out_specs=[pl.BlockSpec((1, l, dk), lambda bi, h: (bi, 0, h)),
                   pl.BlockSpec((1, lc, dk), lambda bi, h: (bi, 0, h))],
        out_shape=[jax.ShapeDtypeStruct((b, l, d), BF16), jax.ShapeDtypeStruct((b, lc, d), BF16)],
        scratch_shapes=[pltpu.VMEM((l, dk), F32), pltpu.VMEM((l, dk), F32), pltpu.VMEM((l, dk), F32),
                        pltpu.VMEM((dk, dk), F32), pltpu.VMEM((dk, dk), F32),
                        pltpu.VMEM((c, c), F32), pltpu.VMEM((c, c), F32), pltpu.VMEM((4, c, dk), BF16)],
        compiler_params=_params(),
        name="retention",
    )(z, z, z, z, z, zc, zc, zc, zc, zc, spread(a_f), spread(a_b),
      gn_f.reshape(1, d).astype(F32), gn_b.reshape(1, d).astype(F32))


def kernel(x, c, ctx, c_ctx, ada_w, ada_b, norm1_g, norm2_g, conv_pw1_w, conv_pw1_b, conv_dw_w, conv_dw_b, conv_ln_g, conv_ln_b, conv_pw2_w, conv_pw2_b, ret_w_in, ret_w_out, ret_decay_f, ret_decay_b, ret_gn_f, ret_gn_b, ffn_w_in, ffn_dw_w, ffn_dw_b, ffn_w_out, final_g):
    b, l, d = x.shape
    depth = ada_w.shape[0]
    n_mixers = 2

    rows = -(-(b + 1) // 8) * 8
    cvec = jnp.concatenate([c, c_ctx[None, :], jnp.zeros((rows - b - 1, d), F32)], axis=0)
    mod = _modulation(cvec, ada_w, ada_b)
    tables = _rope_tables(l, d // N_HEADS)

    hctx = ctx
    for i in range(depth):
        last = i == depth - 1
        j = i // n_mixers
        mod_x = mod[i, :b][:, None, :]
        mod_c = jnp.broadcast_to(mod[i, b][None, None, :], (b, 1, 6 * d))
        pre_x = pre_c = None
        if i % n_mixers == 0:
            cargs = (norm1_g[i], conv_pw1_w[j].astype(BF16), conv_pw1_b[j], conv_dw_w[j], conv_dw_b[j],
                     conv_ln_g[j], conv_ln_b[j], conv_pw2_w[j].astype(BF16), conv_pw2_b[j])
            x = _conv_mixer(x, mod_x, *cargs)
            if not last:
                hctx = _conv_mixer(hctx, mod_c, *cargs)
        else:
            w_in = ret_w_in[j].astype(BF16)
            z = _ret_project(x, mod_x, norm1_g[i], w_in, tables)
            zc = _ret_project(hctx, mod_c, norm1_g[i], w_in)
            y, yc = _retention(z, zc, ret_decay_f[j], ret_decay_b[j], ret_gn_f[j], ret_gn_b[j])
            w_out = ret_w_out[j].astype(BF16)
            pre_x, pre_c = (y, w_out), (yc, w_out)
        fargs = (norm2_g[i], ffn_w_in[i].astype(BF16), ffn_dw_w[i], ffn_dw_b[i], ffn_w_out[i].astype(BF16))
        x = _conv_ffn(x, mod_x, *fargs, pre=pre_x, final_g=final_g if last else None)
        if not last:
            hctx = _conv_ffn(hctx, mod_c, *fargs, pre=pre_c)
    return x
```

```python
import functools

import jax
import jax.numpy as jnp
from jax import lax
from jax.experimental import pallas as pl
from jax.experimental.pallas import tpu as pltpu

F32 = jnp.float32
BF16 = jnp.bfloat16

NORM_EPS = 1e-6
GN_EPS = 1e-5
ROPE_BASE = 10000.0
GRID_WIDTH = 64
N_HEADS = 4
CHUNK = 128
MXU_TILE = 256
LANES = 128
HALO = 16
SEQ_TILE = 512
CONV_ROWS = 32
VMEM_LIMIT = 56 * 1024 * 1024


def _sigmoid(x):
    return 1.0 / (1.0 + jnp.exp(-x))


def _silu(x):
    return x * _sigmoid(x)


def _dot(a, b):
    return jnp.dot(a, b, preferred_element_type=F32)


def _rms(x):
    return x * lax.rsqrt(jnp.mean(x * x, axis=-1, keepdims=True) + NORM_EPS)


def _rms_mod(x, g, shift, scale):
    return _rms(x) * (g * (1.0 + scale)) + shift


def _resident(shape):
    nd = len(shape)
    return pl.BlockSpec(shape, lambda *_: (0,) * nd, pipeline_mode=pl.Buffered(1))


def _seq_specs(t, l, d):
    per = t // HALO
    nh = l // HALO
    return [
        pl.BlockSpec((1, t, d), lambda b, i: (b, i, 0)),
        pl.BlockSpec((1, HALO, d), lambda b, i: (b, jnp.maximum(i * per - 1, 0), 0)),
        pl.BlockSpec((1, HALO, d), lambda b, i: (b, jnp.minimum((i + 1) * per, nh - 1), 0)),
    ]


def _params(n_axes=2):
    return pltpu.CompilerParams(dimension_semantics=("parallel",) * n_axes, vmem_limit_bytes=VMEM_LIMIT)


def _mod_kernel(c_ref, w_ref, b_ref, o_ref):
    s = _silu(c_ref[...]).astype(BF16)
    o_ref[0] = _dot(s, w_ref[0].astype(BF16)) + b_ref[0]


def _modulation(cvec, ada_w, ada_b):
    nl, d, d6 = ada_w.shape
    r = cvec.shape[0]
    return pl.pallas_call(
        _mod_kernel,
        grid=(nl, d6 // d),
        in_specs=[
            pl.BlockSpec((r, d), lambda l, j: (0, 0)),
            pl.BlockSpec((1, d, d), lambda l, j: (l, 0, j)),
            pl.BlockSpec((1, 1, d), lambda l, j: (l, 0, j)),
        ],
        out_specs=pl.BlockSpec((1, r, d), lambda l, j: (l, 0, j)),
        out_shape=jax.ShapeDtypeStruct((nl, r, d6), F32),
        compiler_params=_params(),
        name="adaln_modulation",
    )(cvec, ada_w, ada_b.reshape(nl, 1, d6))


def _conv_kernel(xm_ref, xp_ref, xn_ref, mod_ref, ng_ref, w3_ref, b3_ref, dww_ref, dwb_ref,
                 lng_ref, lnb_ref, pw2_ref, pw2b_ref, o_ref, h_scr, u_scr, v_scr, *, t, d, kw):
    i = pl.program_id(1)
    last = pl.num_programs(1) - 1
    mod = mod_ref[0]
    shift, scale, gate = mod[:, 0:d], mod[:, d:2 * d], mod[:, 2 * d:3 * d]
    g = ng_ref[...]
    te = t + 2 * HALO
    nc = d // LANES

    h_scr[0:HALO, :] = _rms_mod(xp_ref[0], g, shift, scale).astype(BF16)
    h_scr[HALO:HALO + t, :] = _rms_mod(xm_ref[0], g, shift, scale).astype(BF16)
    h_scr[HALO + t:te, :] = _rms_mod(xn_ref[0], g, shift, scale).astype(BF16)

    rows = lax.broadcasted_iota(jnp.int32, (te, 1), 0)
    inside = jnp.logical_and(jnp.logical_or(rows >= HALO, i > 0),
                             jnp.logical_or(rows < HALO + t, i < last))
    off = HALO - kw // 2

    def produce(c, slot):
        z = _dot(h_scr[...], w3_ref[c]) + b3_ref[c]
        u = jnp.where(inside, z[:, :LANES] * _sigmoid(z[:, LANES:]), 0.0)
        u_scr[slot, pl.ds(0, te, stride=2), :] = u

    def conv(c, slot):
        for rb in range(t // CONV_ROWS):
            acc = jnp.zeros((CONV_ROWS, LANES), F32) + dwb_ref[c]
            for k in range(kw):
                tap = u_scr[slot, pl.ds(2 * (rb * CONV_ROWS + off + k), CONV_ROWS, stride=2), :]
                acc = acc + tap * dww_ref[c, k:k + 1, :]
            v_scr[c, rb * CONV_ROWS:(rb + 1) * CONV_ROWS, :] = acc

    produce(0, 0)

    def body(c, carry):
        slot = lax.rem(c, 2)
        conv(c, slot)
        produce(c + 1, 1 - slot)
        return carry

    lax.fori_loop(0, nc - 1, body, 0)
    conv(nc - 1, (nc - 1) % 2)

    v = jnp.concatenate([v_scr[c] for c in range(nc)], axis=1)
    mu = jnp.mean(v, axis=-1, keepdims=True)
    vc = v - mu
    var = jnp.mean(vc * vc, axis=-1, keepdims=True)
    y = vc * lax.rsqrt(var + NORM_EPS) * lng_ref[...] + lnb_ref[...]
    out = _dot(_silu(y).astype(BF16), pw2_ref[...]) + pw2b_ref[...]
    o_ref[0] = xm_ref[0] + gate * out


def _conv_mixer(x, mod, ng, pw1, pw1b, dww, dwb, lng, lnb, pw2, pw2b):
    b, l, d = x.shape
    t = min(SEQ_TILE, l)
    kw = dww.shape[0]
    nc = d // LANES
    assert l % t == 0 and t % CONV_ROWS == 0 and kw // 2 < HALO and d % LANES == 0
    kwp = -(-kw // 8) * 8
    te = t + 2 * HALO
    w3 = jnp.concatenate([pw1[:, :d].reshape(d, nc, LANES), pw1[:, d:].reshape(d, nc, LANES)],
                         axis=2).transpose(1, 0, 2)
    b3 = jnp.concatenate([pw1b[:d].reshape(nc, 1, LANES), pw1b[d:].reshape(nc, 1, LANES)], axis=2)
    dww3 = jnp.pad(dww, ((0, kwp - kw), (0, 0))).reshape(kwp, nc, LANES).transpose(1, 0, 2)
    dwb3 = dwb.reshape(nc, 1, LANES)
    kern = functools.partial(_conv_kernel, t=t, d=d, kw=kw)
    return pl.pallas_call(
        kern,
        grid=(b, l // t),
        in_specs=_seq_specs(t, l, d) + [
            pl.BlockSpec((1, 1, 6 * d), lambda bi, i: (bi, 0, 0)),
            _resident((1, d)),
            _resident((nc, d, 2 * LANES)), _resident((nc, 1, 2 * LANES)),
            _resident((nc, kwp, LANES)), _resident((nc, 1, LANES)),
            _resident((1, d)), _resident((1, d)),
            _resident((d, d)), _resident((1, d)),
        ],
        out_specs=pl.BlockSpec((1, t, d), lambda bi, i: (bi, i, 0)),
        out_shape=jax.ShapeDtypeStruct((b, l, d), F32),
        scratch_shapes=[
            pltpu.VMEM((te, d), BF16),
            pltpu.VMEM((2, 2 * te, LANES), F32),
            pltpu.VMEM((nc, t, LANES), F32),
        ],
        compiler_params=_params(),
        name="conv_mixer",
    )(x, x, x, mod, ng.reshape(1, d), w3, b3, dww3, dwb3,
      lng.reshape(1, d), lnb.reshape(1, d), pw2, pw2b.reshape(1, d))


def _ffn_kernel(*refs, t, d, f, pre_proj, final_norm):
    refs = list(refs)
    xm_ref, xp_ref, xn_ref = refs[:3]
    pos = 3
    if pre_proj:
        ym_ref, yp_ref, yn_ref, wo_ref = refs[pos:pos + 4]
        pos += 4
    mod_ref, ng_ref, win_ref, dww_ref, dwb_ref, wout_ref = refs[pos:pos + 6]
    pos += 6
    if final_norm:
        fg_ref = refs[pos]
        pos += 1
    o_ref = refs[pos]
    scr = refs[pos + 1:]
    h_scr, g_scr, a_scr = scr[:3]

    i = pl.program_id(1)
    last = pl.num_programs(1) - 1
    mod = mod_ref[0]
    shift, scale, gate2 = mod[:, 3 * d:4 * d], mod[:, 4 * d:5 * d], mod[:, 5 * d:6 * d]
    g = ng_ref[...]
    te = t + 2 * HALO

    xp, xm, xn = xp_ref[0], xm_ref[0], xn_ref[0]
    if pre_proj:
        y_scr = scr[3]
        gate1 = mod[:, 2 * d:3 * d]
        y_scr[0:HALO, :] = yp_ref[0]
        y_scr[HALO:HALO + t, :] = ym_ref[0]
        y_scr[HALO + t:te, :] = yn_ref[0]
        proj = _dot(y_scr[...], wo_ref[...])
        xp = xp + gate1 * proj[0:HALO]
        xm = xm + gate1 * proj[HALO:HALO + t]
        xn = xn + gate1 * proj[HALO + t:te]
    o_ref[0] = xm

    hp = _rms_mod(xp, g, shift, scale)
    hn = _rms_mod(xn, g, shift, scale)
    h_scr[0:HALO, :] = jnp.where(i > 0, hp, 0.0).astype(BF16)
    h_scr[HALO:HALO + t, :] = _rms_mod(xm, g, shift, scale).astype(BF16)
    h_scr[HALO + t:te, :] = jnp.where(i < last, hn, 0.0).astype(BF16)

    g_scr[...] = _dot(h_scr[...], win_ref[:, f:2 * f])
    gc = (g_scr[pl.ds(HALO - 1, t), :] * dww_ref[0:1, :]
          + g_scr[pl.ds(HALO, t), :] * dww_ref[1:2, :]
          + g_scr[pl.ds(HALO + 1, t), :] * dww_ref[2:3, :]
          + dwb_ref[...])
    u = _dot(h_scr[pl.ds(HALO, t), :], win_ref[:, 0:f])
    a_scr[...] = (u * _silu(gc)).astype(BF16)

    res = o_ref[0] + gate2 * _dot(a_scr[...], wout_ref[...])
    if final_norm:
        res = _rms(res) * fg_ref[...]
    o_ref[0] = res


def _conv_ffn(x, mod, ng, win, dww, dwb, wout, pre=None, final_g=None):
    b, l, d = x.shape
    f = wout.shape[0]
    t = min(SEQ_TILE, l)
    assert l % t == 0 and t % HALO == 0 and dww.shape[0] == 3
    te = t + 2 * HALO
    seq = _seq_specs(t, l, d)
    in_specs = list(seq)
    args = [x, x, x]
    scratch = [pltpu.VMEM((te, d), BF16), pltpu.VMEM((te, f), F32), pltpu.VMEM((t, f), BF16)]
    if pre is not None:
        y, wo = pre
        in_specs += list(seq) + [_resident(wo.shape)]
        args += [y, y, y, wo]
        scratch.append(pltpu.VMEM((te, d), BF16))
    in_specs += [
        pl.BlockSpec((1, 1, 6 * d), lambda bi, i: (bi, 0, 0)),
        _resident((1, d)),
        _resident((d, 2 * f)),
        _resident((8, f)), _resident((1, f)),
        _resident((f, d)),
    ]
    args += [mod, ng.reshape(1, d), win, jnp.pad(dww, ((0, 5), (0, 0))), dwb.reshape(1, f), wout]
    if final_g is not None:
        in_specs.append(_resident((1, d)))
        args.append(final_g.reshape(1, d))
    kern = functools.partial(_ffn_kernel, t=t, d=d, f=f, pre_proj=pre is not None,
                             final_norm=final_g is not None)
    return pl.pallas_call(
        kern,
        grid=(b, l // t),
        in_specs=in_specs,
        out_specs=pl.BlockSpec((1, t, d), lambda bi, i: (bi, i, 0)),
        out_shape=jax.ShapeDtypeStruct((b, l, d), F32),
        scratch_shapes=scratch,
        compiler_params=_params(),
        name="conv_ffn",
    )(*args)


def _proj_kernel(*refs, d, rope):
    if rope:
        x_ref, mod_ref, ng_ref, w_ref, cos_ref, sin_ref, o_ref = refs
    else:
        x_ref, mod_ref, ng_ref, w_ref, o_ref = refs
    mod = mod_ref[0]
    shift, scale = mod[:, 0:d], mod[:, d:2 * d]
    h = _rms_mod(x_ref[0], ng_ref[...], shift, scale).astype(BF16)
    dk = d // N_HEADS
    kscale = dk ** -0.5
    for part in (3, 4, 0, 1, 2):
        z = _dot(h, w_ref[:, part * d:(part + 1) * d])
        if part == 1:
            z = z * kscale
        if rope and part < 2:
            for g in range(d // LANES):
                tab = slice((g % (dk // LANES)) * LANES, (g % (dk // LANES) + 1) * LANES)
                zg = z[:, g * LANES:(g + 1) * LANES]
                sw = pltpu.roll(zg, LANES // 2, 1)
                o_ref[0, :, part * d + g * LANES:part * d + (g + 1) * LANES] = (
                    zg * cos_ref[:, tab] + sw * sin_ref[:, tab]).astype(BF16)
        else:
            if part >= 3:
                z = _silu(z)
            o_ref[0, :, part * d:(part + 1) * d] = z.astype(BF16)


def _ret_project(x, mod, ng, w_in, tables=None):
    b, l, d = x.shape
    t = min(SEQ_TILE, l)
    assert l % t == 0 and w_in.shape[1] == 5 * d and d // N_HEADS == 2 * LANES
    dk = d // N_HEADS
    in_specs = [
        pl.BlockSpec((1, t, d), lambda bi, i: (bi, i, 0)),
        pl.BlockSpec((1, 1, 6 * d), lambda bi, i: (bi, 0, 0)),
        _resident((1, d)),
        _resident((d, 5 * d)),
    ]
    args = [x, mod, ng.reshape(1, d), w_in]
    if tables is not None:
        in_specs += [pl.BlockSpec((t, dk), lambda bi, i: (i, 0))] * 2
        args += list(tables)
    return pl.pallas_call(
        functools.partial(_proj_kernel, d=d, rope=tables is not None),
        grid=(b, l // t),
        in_specs=in_specs,
        out_specs=pl.BlockSpec((1, t, 5 * d), lambda bi, i: (bi, i, 0)),
        out_shape=jax.ShapeDtypeStruct((b, l, 5 * d), BF16),
        compiler_params=_params(),
        name="ret_project",
    )(*args)


def _rope_tables(l, dk):
    half = dk // 4
    pos = jnp.arange(l, dtype=jnp.int32)
    row = (pos // GRID_WIDTH).astype(F32)
    col = (pos % GRID_WIDTH).astype(F32)
    freqs = ROPE_BASE ** (-jnp.arange(half, dtype=F32) / half)
    cos, sin = [], []
    for p in (row, col):
        ang = p[:, None] * freqs[None, :]
        cos += [jnp.cos(ang), jnp.cos(ang)]
        sin += [-jnp.sin(ang), jnp.sin(ang)]
    return jnp.concatenate(cos, axis=-1), jnp.concatenate(sin, axis=-1)


def _ret_kernel(q_ref, k_ref, v_ref, gf_ref, gb_ref, qc_ref, kc_ref, vc_ref, gfc_ref, gbc_ref,
                af_ref, ab_ref, gnf_ref, gnb_ref, y_ref, yc_ref,
                of_scr, ob_scr, y_scr, sf_scr, sb_scr, df_scr, db_scr, tab_scr,
                *, n_lat, n_ctx):
    c = df_scr.shape[0]
    dv = sf_scr.shape[1]
    lgf = -jnp.exp(af_ref[0])
    lgb = -jnp.exp(ab_ref[0])
    ri = lax.broadcasted_iota(jnp.int32, (c, c), 0)
    ci = lax.broadcasted_iota(jnp.int32, (c, c), 1)
    diff = (ri - ci).astype(F32)
    df_scr[...] = jnp.where(diff >= 0, jnp.exp(jnp.maximum(diff, 0.0) * lgf[:, :c]), 0.0)
    db_scr[...] = jnp.where(diff <= 0, jnp.exp(jnp.maximum(-diff, 0.0) * lgb[:, :c]), 0.0)
    idx = lax.broadcasted_iota(jnp.int32, (c, dv), 0).astype(F32)
    tab_scr[0] = jnp.exp((idx + 1.0) * lgf).astype(BF16)
    tab_scr[1] = jnp.exp((c - 1.0 - idx) * lgf).astype(BF16)
    tab_scr[2] = jnp.exp((c - idx) * lgb).astype(BF16)
    tab_scr[3] = jnp.exp(idx * lgb).astype(BF16)
    gc_f = jnp.exp(c * lgf)
    gc_b = jnp.exp(c * lgb)
    gnf = gnf_ref[...]
    gnb = gnb_ref[...]
    tn = (((0,), (0,)), ((), ()))
    nt = (((1,), (1,)), ((), ()))

    def group_norm(o):
        mu = jnp.mean(o, axis=-1, keepdims=True)
        oc = o - mu
        var = jnp.mean(oc * oc, axis=-1, keepdims=True)
        return oc * lax.rsqrt(var + GN_EPS)

    def scan(q_r, k_r, v_r, gf_r, gb_r, out_r, n_chunks):
        def rows_of(n):
            return pl.ds(pl.multiple_of(n * c, c), c)

        def mm_fwd(n):
            rows = rows_of(n)
            q, k, v = q_r[0, rows, :], k_r[0, rows, :], v_r[0, rows, :]
            a = lax.dot_general(q, k, nt, preferred_element_type=F32)
            s = sf_scr[...]
            of_scr[rows, :] = _dot((a * df_scr[...]).astype(BF16), v) + _dot(q * tab_scr[0], s.astype(BF16))
            sf_scr[...] = gc_f * s + lax.dot_general(k * tab_scr[1], v, tn, preferred_element_type=F32)

        def mm_bwd(n):
            rows = rows_of(n)
            q, k, v = q_r[0, rows, :], k_r[0, rows, :], v_r[0, rows, :]
            a = lax.dot_general(q, k, nt, preferred_element_type=F32)
            s = sb_scr[...]
            ob_scr[rows, :] = _dot((a * db_scr[...]).astype(BF16), v) + _dot(q * tab_scr[2], s.astype(BF16))
            sb_scr[...] = gc_b * s + lax.dot_general(k * tab_scr[3], v, tn, preferred_element_type=F32)

        def gate(n, raw_scr, gate_r, gn, first):
            rows = rows_of(n)
            term = gate_r[0, rows, :].astype(F32) * (group_norm(raw_scr[rows, :]) * gn)
            if first:
                y_scr[rows, :] = term
            else:
                out_r[0, rows, :] = (y_scr[rows, :] + term).astype(BF16)

        if n_chunks == 1:
            mm_fwd(0)
            mm_bwd(0)
            gate(0, of_scr, gf_r, gnf, True)
            gate(0, ob_scr, gb_r, gnb, False)
            return
        assert n_chunks % 2 == 0
        half = n_chunks // 2
        mm_fwd(0)
        mm_bwd(n_chunks - 1)

        def step(s, first):
            gate(s - 1, of_scr, gf_r, gnf, first)
            gate(n_chunks - s, ob_scr, gb_r, gnb, first)
            mm_fwd(s)
            mm_bwd(n_chunks - 1 - s)

        def first_half(s, carry):
            step(s, True)
            return carry

        def second_half(s, carry):
            step(s, False)
            return carry

        lax.fori_loop(1, half + 1, first_half, 0, unroll=2)
        lax.fori_loop(half + 1, n_chunks, second_half, 0, unroll=2)
        gate(n_chunks - 1, of_scr, gf_r, gnf, False)
        gate(0, ob_scr, gb_r, gnb, False)

    sf_scr[...] = jnp.zeros_like(sf_scr)
    sb_scr[...] = jnp.zeros_like(sb_scr)
    scan(qc_ref, kc_ref, vc_ref, gfc_ref, gbc_ref, yc_ref, n_ctx)
    scan(q_ref, k_ref, v_ref, gf_ref, gb_ref, y_ref, n_lat)


def _ret_chunk(l, lc):
    for c in (MXU_TILE, CHUNK):
        if all(n % c == 0 and (n // c == 1 or (n // c) % 2 == 0) for n in (l, lc)):
            return c
    raise ValueError(f"unsupported sequence lengths {l}, {lc}")


def _retention(z, zc, a_f, a_b, gn_f, gn_b):
    b, l, d5 = z.shape
    lc = zc.shape[1]
    d = d5 // 5
    dk = d // N_HEADS
    c = _ret_chunk(l, lc)
    assert lc <= l

    def part_specs(n_rows):
        return [pl.BlockSpec((1, n_rows, dk), lambda bi, h, p=p: (bi, 0, p * N_HEADS + h)) for p in range(5)]

    head_vec = pl.BlockSpec((1, 1, dk), lambda bi, h: (h, 0, 0))
    gn_spec = pl.BlockSpec((1, dk), lambda bi, h: (0, h))
    kern = functools.partial(_ret_kernel, n_lat=l // c, n_ctx=lc // c)
    spread = lambda a: jnp.broadcast_to(a.astype(F32)[:, None, None], (N_HEADS, 1, dk))
    return pl.pallas_call(
        kern,
        grid=(b, N_HEADS),
        in_specs=part_specs(l) + part_specs(lc) + [head_vec, head_vec, gn_spec, gn_spec],
        out_specs=[pl.BlockSpec((1, l, dk), lambda bi, h: (bi, 0, h)),
                   pl.BlockSpec((1, lc, dk), lambda bi, h: (bi, 0, h))],
        out_shape=[jax.ShapeDtypeStruct((b, l, d), BF16), jax.ShapeDtypeStruct((b, lc, d), BF16)],
        scratch_shapes=[pltpu.VMEM((l, dk), F32), pltpu.VMEM((l, dk), F32), pltpu.VMEM((l, dk), F32),
                        pltpu.VMEM((dk, dk), F32), pltpu.VMEM((dk, dk), F32),
                        pltpu.VMEM((c, c), F32), pltpu.VMEM((c, c), F32), pltpu.VMEM((4, c, dk), BF16)],
        compiler_params=_params(),
        name="retention",
    )(z, z, z, z, z, zc, zc, zc, zc, zc, spread(a_f), spread(a_b),
      gn_f.reshape(1, d).astype(F32), gn_b.reshape(1, d).astype(F32))


def kernel(x, c, ctx, c_ctx, ada_w, ada_b, norm1_g, norm2_g, conv_pw1_w, conv_pw1_b, conv_dw_w, conv_dw_b, conv_ln_g, conv_ln_b, conv_pw2_w, conv_pw2_b, ret_w_in, ret_w_out, ret_decay_f, ret_decay_b, ret_gn_f, ret_gn_b, ffn_w_in, ffn_dw_w, ffn_dw_b, ffn_w_out, final_g):
    b, l, d = x.shape
    depth = ada_w.shape[0]
    n_mixers = 2

    rows = -(-(b + 1) // 8) * 8
    cvec = jnp.concatenate([c, c_ctx[None, :], jnp.zeros((rows - b - 1, d), F32)], axis=0)
    mod = _modulation(cvec, ada_w, ada_b)
    tables = _rope_tables(l, d // N_HEADS)

    hctx = ctx
    for i in range(depth):
        last = i == depth - 1
        j = i // n_mixers
        mod_x = mod[i, :b][:, None, :]
        mod_c = jnp.broadcast_to(mod[i, b][None, None, :], (b, 1, 6 * d))
        pre_x = pre_c = None
        if i % n_mixers == 0:
            cargs = (norm1_g[i], conv_pw1_w[j].astype(BF16), conv_pw1_b[j], conv_dw_w[j], conv_dw_b[j],
                     conv_ln_g[j], conv_ln_b[j], conv_pw2_w[j].astype(BF16), conv_pw2_b[j])
            x = _conv_mixer(x, mod_x, *cargs)
            if not last:
                hctx = _conv_mixer(hctx, mod_c, *cargs)
        else:
            w_in = ret_w_in[j].astype(BF16)
            z = _ret_project(x, mod_x, norm1_g[i], w_in, tables)
            zc = _ret_project(hctx, mod_c, norm1_g[i], w_in)
            y, yc = _retention(z, zc, ret_decay_f[j], ret_decay_b[j], ret_gn_f[j], ret_gn_b[j])
            w_out = ret_w_out[j].astype(BF16)
            pre_x, pre_c = (y, w_out), (yc, w_out)
        fargs = (norm2_g[i], ffn_w_in[i].astype(BF16), ffn_dw_w[i], ffn_dw_b[i], ffn_w_out[i].astype(BF16))
        x = _conv_ffn(x, mod_x, *fargs, pre=pre_x, final_g=final_g if last else None)
        if not last:
            hctx = _conv_ffn(hctx, mod_c, *fargs, pre=pre_c)
    return x
```

```python
import functools

import jax
import jax.numpy as jnp
from jax import lax
from jax.experimental import pallas as pl
from jax.experimental.pallas import tpu as pltpu

F32 = jnp.float32
BF16 = jnp.bfloat16

NORM_EPS = 1e-6
GN_EPS = 1e-5
ROPE_BASE = 10000.0
GRID_WIDTH = 64
N_HEADS = 4
CHUNK = 128
MXU_TILE = 256
LANES = 128
HALO = 16
SEQ_TILE = 512
CONV_ROWS = 32
VMEM_LIMIT = 56 * 1024 * 1024


def _sigmoid(x):
    return 0.5 + 0.5 * jnp.tanh(0.5 * x)


def _silu(x):
    half = 0.5 * x
    return half + half * jnp.tanh(half)


def _dot(a, b):
    return jnp.dot(a, b, preferred_element_type=F32)


def _rms(x):
    return x * lax.rsqrt(jnp.mean(x * x, axis=-1, keepdims=True) + NORM_EPS)


def _rms_mod(x, g, shift, scale):
    return _rms(x) * (g * (1.0 + scale)) + shift


def _resident(shape):
    nd = len(shape)
    return pl.BlockSpec(shape, lambda *_: (0,) * nd, pipeline_mode=pl.Buffered(1))


def _seq_specs(t, l, d):
    per = t // HALO
    nh = l // HALO
    return [
        pl.BlockSpec((1, t, d), lambda b, i: (b, i, 0)),
        pl.BlockSpec((1, HALO, d), lambda b, i: (b, jnp.maximum(i * per - 1, 0), 0)),
        pl.BlockSpec((1, HALO, d), lambda b, i: (b, jnp.minimum((i + 1) * per, nh - 1), 0)),
    ]


def _params(n_axes=2):
    return pltpu.CompilerParams(dimension_semantics=("parallel",) * n_axes, vmem_limit_bytes=VMEM_LIMIT)


def _mod_kernel(c_ref, w_ref, b_ref, o_ref):
    s = _silu(c_ref[...]).astype(BF16)
    o_ref[0] = _dot(s, w_ref[0].astype(BF16)) + b_ref[0]


def _modulation(cvec, ada_w, ada_b):
    nl, d, d6 = ada_w.shape
    r = cvec.shape[0]
    return pl.pallas_call(
        _mod_kernel,
        grid=(nl, d6 // d),
        in_specs=[
            pl.BlockSpec((r, d), lambda l, j: (0, 0)),
            pl.BlockSpec((1, d, d), lambda l, j: (l, 0, j)),
            pl.BlockSpec((1, 1, d), lambda l, j: (l, 0, j)),
        ],
        out_specs=pl.BlockSpec((1, r, d), lambda l, j: (l, 0, j)),
        out_shape=jax.ShapeDtypeStruct((nl, r, d6), F32),
        compiler_params=_params(),
        name="adaln_modulation",
    )(cvec, ada_w, ada_b.reshape(nl, 1, d6))


def _conv_kernel(xm_ref, xp_ref, xn_ref, mod_ref, ng_ref, w3_ref, b3_ref, dww_ref, dwb_ref,
                 lng_ref, lnb_ref, pw2_ref, pw2b_ref, o_ref, h_scr, u_scr, v_scr, *, t, d, kw):
    i = pl.program_id(1)
    last = pl.num_programs(1) - 1
    mod = mod_ref[0]
    shift, scale, gate = mod[:, 0:d], mod[:, d:2 * d], mod[:, 2 * d:3 * d]
    g = ng_ref[...]
    te = t + 2 * HALO
    nc = d // LANES

    h_scr[0:HALO, :] = _rms_mod(xp_ref[0], g, shift, scale).astype(BF16)
    h_scr[HALO:HALO + t, :] = _rms_mod(xm_ref[0], g, shift, scale).astype(BF16)
    h_scr[HALO + t:te, :] = _rms_mod(xn_ref[0], g, shift, scale).astype(BF16)

    rows = lax.broadcasted_iota(jnp.int32, (te, 1), 0)
    inside = jnp.logical_and(jnp.logical_or(rows >= HALO, i > 0),
                             jnp.logical_or(rows < HALO + t, i < last))
    off = HALO - kw // 2

    def produce(c, slot):
        z = _dot(h_scr[...], w3_ref[c]) + b3_ref[c]
        u = jnp.where(inside, z[:, :LANES] * _sigmoid(z[:, LANES:]), 0.0)
        u_scr[slot, pl.ds(0, te, stride=2), :] = u

    def conv(c, slot):
        for rb in range(t // CONV_ROWS):
            acc = jnp.zeros((CONV_ROWS, LANES), F32) + dwb_ref[c]
            for k in range(kw):
                tap = u_scr[slot, pl.ds(2 * (rb * CONV_ROWS + off + k), CONV_ROWS, stride=2), :]
                acc = acc + tap * dww_ref[c, k:k + 1, :]
            v_scr[c, rb * CONV_ROWS:(rb + 1) * CONV_ROWS, :] = acc

    produce(0, 0)

    def body(c, carry):
        slot = lax.rem(c, 2)
        conv(c, slot)
        produce(c + 1, 1 - slot)
        return carry

    lax.fori_loop(0, nc - 1, body, 0)
    conv(nc - 1, (nc - 1) % 2)

    v = jnp.concatenate([v_scr[c] for c in range(nc)], axis=1)
    mu = jnp.mean(v, axis=-1, keepdims=True)
    vc = v - mu
    var = jnp.mean(vc * vc, axis=-1, keepdims=True)
    y = vc * lax.rsqrt(var + NORM_EPS) * lng_ref[...] + lnb_ref[...]
    out = _dot(_silu(y).astype(BF16), pw2_ref[...]) + pw2b_ref[...]
    o_ref[0] = xm_ref[0] + gate * out


def _conv_mixer(x, mod, ng, pw1, pw1b, dww, dwb, lng, lnb, pw2, pw2b):
    b, l, d = x.shape
    t = min(SEQ_TILE, l)
    kw = dww.shape[0]
    nc = d // LANES
    assert l % t == 0 and t % CONV_ROWS == 0 and kw // 2 < HALO and d % LANES == 0
    kwp = -(-kw // 8) * 8
    te = t + 2 * HALO
    w3 = jnp.concatenate([pw1[:, :d].reshape(d, nc, LANES), pw1[:, d:].reshape(d, nc, LANES)],
                         axis=2).transpose(1, 0, 2)
    b3 = jnp.concatenate([pw1b[:d].reshape(nc, 1, LANES), pw1b[d:].reshape(nc, 1, LANES)], axis=2)
    dww3 = jnp.pad(dww, ((0, kwp - kw), (0, 0))).reshape(kwp, nc, LANES).transpose(1, 0, 2)
    dwb3 = dwb.reshape(nc, 1, LANES)
    kern = functools.partial(_conv_kernel, t=t, d=d, kw=kw)
    return pl.pallas_call(
        kern,
        grid=(b, l // t),
        in_specs=_seq_specs(t, l, d) + [
            pl.BlockSpec((1, 1, 6 * d), lambda bi, i: (bi, 0, 0)),
            _resident((1, d)),
            _resident((nc, d, 2 * LANES)), _resident((nc, 1, 2 * LANES)),
            _resident((nc, kwp, LANES)), _resident((nc, 1, LANES)),
            _resident((1, d)), _resident((1, d)),
            _resident((d, d)), _resident((1, d)),
        ],
        out_specs=pl.BlockSpec((1, t, d), lambda bi, i: (bi, i, 0)),
        out_shape=jax.ShapeDtypeStruct((b, l, d), F32),
        scratch_shapes=[
            pltpu.VMEM((te, d), BF16),
            pltpu.VMEM((2, 2 * te, LANES), F32),
            pltpu.VMEM((nc, t, LANES), F32),
        ],
        compiler_params=_params(),
        name="conv_mixer",
    )(x, x, x, mod, ng.reshape(1, d), w3, b3, dww3, dwb3,
      lng.reshape(1, d), lnb.reshape(1, d), pw2, pw2b.reshape(1, d))


def _ffn_kernel(*refs, t, d, f, pre_proj, final_norm):
    refs = list(refs)
    xm_ref, xp_ref, xn_ref = refs[:3]
    pos = 3
    if pre_proj:
        ym_ref, yp_ref, yn_ref, wo_ref = refs[pos:pos + 4]
        pos += 4
    mod_ref, ng_ref, win_ref, dww_ref, dwb_ref, wout_ref = refs[pos:pos + 6]
    pos += 6
    if final_norm:
        fg_ref = refs[pos]
        pos += 1
    o_ref = refs[pos]
    scr = refs[pos + 1:]
    h_scr, g_scr, a_scr = scr[:3]

    i = pl.program_id(1)
    last = pl.num_programs(1) - 1
    mod = mod_ref[0]
    shift, scale, gate2 = mod[:, 3 * d:4 * d], mod[:, 4 * d:5 * d], mod[:, 5 * d:6 * d]
    g = ng_ref[...]
    te = t + 2 * HALO

    xp, xm, xn = xp_ref[0], xm_ref[0], xn_ref[0]
    if pre_proj:
        y_scr = scr[3]
        gate1 = mod[:, 2 * d:3 * d]
        y_scr[0:HALO, :] = yp_ref[0]
        y_scr[HALO:HALO + t, :] = ym_ref[0]
        y_scr[HALO + t:te, :] = yn_ref[0]
        proj = _dot(y_scr[...], wo_ref[...])
        xp = xp + gate1 * proj[0:HALO]
        xm = xm + gate1 * proj[HALO:HALO + t]
        xn = xn + gate1 * proj[HALO + t:te]
    o_ref[0] = xm

    hp = _rms_mod(xp, g, shift, scale)
    hn = _rms_mod(xn, g, shift, scale)
    h_scr[0:HALO, :] = jnp.where(i > 0, hp, 0.0).astype(BF16)
    h_scr[HALO:HALO + t, :] = _rms_mod(xm, g, shift, scale).astype(BF16)
    h_scr[HALO + t:te, :] = jnp.where(i < last, hn, 0.0).astype(BF16)

    g_scr[...] = _dot(h_scr[...], win_ref[:, f:2 * f])
    gc = (g_scr[pl.ds(HALO - 1, t), :] * dww_ref[0:1, :]
          + g_scr[pl.ds(HALO, t), :] * dww_ref[1:2, :]
          + g_scr[pl.ds(HALO + 1, t), :] * dww_ref[2:3, :]
          + dwb_ref[...])
    u = _dot(h_scr[pl.ds(HALO, t), :], win_ref[:, 0:f])
    a_scr[...] = (u * _silu(gc)).astype(BF16)

    res = o_ref[0] + gate2 * _dot(a_scr[...], wout_ref[...])
    if final_norm:
        res = _rms(res) * fg_ref[...]
    o_ref[0] = res


def _conv_ffn(x, mod, ng, win, dww, dwb, wout, pre=None, final_g=None):
    b, l, d = x.shape
    f = wout.shape[0]
    t = min(SEQ_TILE, l)
    assert l % t == 0 and t % HALO == 0 and dww.shape[0] == 3
    te = t + 2 * HALO
    seq = _seq_specs(t, l, d)
    in_specs = list(seq)
    args = [x, x, x]
    scratch = [pltpu.VMEM((te, d), BF16), pltpu.VMEM((te, f), F32), pltpu.VMEM((t, f), BF16)]
    if pre is not None:
        y, wo = pre
        in_specs += list(seq) + [_resident(wo.shape)]
        args += [y, y, y, wo]
        scratch.append(pltpu.VMEM((te, d), BF16))
    in_specs += [
        pl.BlockSpec((1, 1, 6 * d), lambda bi, i: (bi, 0, 0)),
        _resident((1, d)),
        _resident((d, 2 * f)),
        _resident((8, f)), _resident((1, f)),
        _resident((f, d)),
    ]
    args += [mod, ng.reshape(1, d), win, jnp.pad(dww, ((0, 5), (0, 0))), dwb.reshape(1, f), wout]
    if final_g is not None:
        in_specs.append(_resident((1, d)))
        args.append(final_g.reshape(1, d))
    kern = functools.partial(_ffn_kernel, t=t, d=d, f=f, pre_proj=pre is not None,
                             final_norm=final_g is not None)
    return pl.pallas_call(
        kern,
        grid=(b, l // t),
        in_specs=in_specs,
        out_specs=pl.BlockSpec((1, t, d), lambda bi, i: (bi, i, 0)),
        out_shape=jax.ShapeDtypeStruct((b, l, d), F32),
        scratch_shapes=scratch,
        compiler_params=_params(),
        name="conv_ffn",
    )(*args)


def _proj_kernel(*refs, d, rope):
    if rope:
        x_ref, mod_ref, ng_ref, w_ref, cos_ref, sin_ref, o_ref = refs
    else:
        x_ref, mod_ref, ng_ref, w_ref, o_ref = refs
    mod = mod_ref[0]
    shift, scale = mod[:, 0:d], mod[:, d:2 * d]
    h = _rms_mod(x_ref[0], ng_ref[...], shift, scale).astype(BF16)
    dk = d // N_HEADS
    kscale = dk ** -0.5
    for part in (3, 4, 0, 1, 2):
        z = _dot(h, w_ref[:, part * d:(part + 1) * d])
        if part == 1:
            z = z * kscale
        if rope and part < 2:
            for g in range(d // LANES):
                tab = slice((g % (dk // LANES)) * LANES, (g % (dk // LANES) + 1) * LANES)
                zg = z[:, g * LANES:(g + 1) * LANES]
                sw = pltpu.roll(zg, LANES // 2, 1)
                o_ref[0, :, part * d + g * LANES:part * d + (g + 1) * LANES] = (
                    zg * cos_ref[:, tab] + sw * sin_ref[:, tab]).astype(BF16)
        else:
            if part >= 3:
                z = _silu(z)
            o_ref[0, :, part * d:(part + 1) * d] = z.astype(BF16)


def _ret_project(x, mod, ng, w_in, tables=None):
    b, l, d = x.shape
    t = min(SEQ_TILE, l)
    assert l % t == 0 and w_in.shape[1] == 5 * d and d // N_HEADS == 2 * LANES
    dk = d // N_HEADS
    in_specs = [
        pl.BlockSpec((1, t, d), lambda bi, i: (bi, i, 0)),
        pl.BlockSpec((1, 1, 6 * d), lambda bi, i: (bi, 0, 0)),
        _resident((1, d)),
        _resident((d, 5 * d)),
    ]
    args = [x, mod, ng.reshape(1, d), w_in]
    if tables is not None:
        in_specs += [pl.BlockSpec((t, dk), lambda bi, i: (i, 0))] * 2
        args += list(tables)
    return pl.pallas_call(
        functools.partial(_proj_kernel, d=d, rope=tables is not None),
        grid=(b, l // t),
        in_specs=in_specs,
        out_specs=pl.BlockSpec((1, t, 5 * d), lambda bi, i: (bi, i, 0)),
        out_shape=jax.ShapeDtypeStruct((b, l, 5 * d), BF16),
        compiler_params=_params(),
        name="ret_project",
    )(*args)


def _rope_tables(l, dk):
    half = dk // 4
    pos = jnp.arange(l, dtype=jnp.int32)
    row = (pos // GRID_WIDTH).astype(F32)
    col = (pos % GRID_WIDTH).astype(F32)
    freqs = ROPE_BASE ** (-jnp.arange(half, dtype=F32) / half)
    cos, sin = [], []
    for p in (row, col):
        ang = p[:, None] * freqs[None, :]
        cos += [jnp.cos(ang), jnp.cos(ang)]
        sin += [-jnp.sin(ang), jnp.sin(ang)]
    return jnp.concatenate(cos, axis=-1), jnp.concatenate(sin, axis=-1)


def _ret_kernel(q_ref, k_ref, v_ref, gf_ref, gb_ref, qc_ref, kc_ref, vc_ref, gfc_ref, gbc_ref,
                af_ref, ab_ref, gnf_ref, gnb_ref, y_ref, yc_ref,
                of_scr, ob_scr, y_scr, sf_scr, sb_scr, df_scr, db_scr, tab_scr,
                *, n_lat, n_ctx):
    c = df_scr.shape[0]
    dv = sf_scr.shape[1]
    lgf = -jnp.exp(af_ref[0])
    lgb = -jnp.exp(ab_ref[0])
    ri = lax.broadcasted_iota(jnp.int32, (c, c), 0)
    ci = lax.broadcasted_iota(jnp.int32, (c, c), 1)
    diff = (ri - ci).astype(F32)
    df_scr[...] = jnp.where(diff >= 0, jnp.exp(jnp.maximum(diff, 0.0) * lgf[:, :c]), 0.0)
    db_scr[...] = jnp.where(diff <= 0, jnp.exp(jnp.maximum(-diff, 0.0) * lgb[:, :c]), 0.0)
    idx = lax.broadcasted_iota(jnp.int32, (c, dv), 0).astype(F32)
    tab_scr[0] = jnp.exp((idx + 1.0) * lgf).astype(BF16)
    tab_scr[1] = jnp.exp((c - 1.0 - idx) * lgf).astype(BF16)
    tab_scr[2] = jnp.exp((c - idx) * lgb).astype(BF16)
    tab_scr[3] = jnp.exp(idx * lgb).astype(BF16)
    gc_f = jnp.exp(c * lgf)
    gc_b = jnp.exp(c * lgb)
    gnf = gnf_ref[...]
    gnb = gnb_ref[...]
    tn = (((0,), (0,)), ((), ()))
    nt = (((1,), (1,)), ((), ()))

    def group_norm(o):
        mu = jnp.mean(o, axis=-1, keepdims=True)
        oc = o - mu
        var = jnp.mean(oc * oc, axis=-1, keepdims=True)
        return oc * lax.rsqrt(var + GN_EPS)

    def scan(q_r, k_r, v_r, gf_r, gb_r, out_r, n_chunks):
        def rows_of(n):
            return pl.ds(pl.multiple_of(n * c, c), c)

        def mm_fwd(n):
            rows = rows_of(n)
            q, k, v = q_r[0, rows, :], k_r[0, rows, :], v_r[0, rows, :]
            a = lax.dot_general(q, k, nt, preferred_element_type=F32)
            s = sf_scr[...]
            of_scr[rows, :] = _dot((a * df_scr[...]).astype(BF16), v) + _dot(q * tab_scr[0], s.astype(BF16))
            sf_scr[...] = gc_f * s + lax.dot_general(k * tab_scr[1], v, tn, preferred_element_type=F32)

        def mm_bwd(n):
            rows = rows_of(n)
            q, k, v = q_r[0, rows, :], k_r[0, rows, :], v_r[0, rows, :]
            a = lax.dot_general(q, k, nt, preferred_element_type=F32)
            s = sb_scr[...]
            ob_scr[rows, :] = _dot((a * db_scr[...]).astype(BF16), v) + _dot(q * tab_scr[2], s.astype(BF16))
            sb_scr[...] = gc_b * s + lax.dot_general(k * tab_scr[3], v, tn, preferred_element_type=F32)

        def gate(n, raw_scr, gate_r, gn, first):
            rows = rows_of(n)
            term = gate_r[0, rows, :].astype(F32) * (group_norm(raw_scr[rows, :]) * gn)
            if first:
                y_scr[rows, :] = term
            else:
                out_r[0, rows, :] = (y_scr[rows, :] + term).astype(BF16)

        if n_chunks == 1:
            mm_fwd(0)
            mm_bwd(0)
            gate(0, of_scr, gf_r, gnf, True)
            gate(0, ob_scr, gb_r, gnb, False)
            return
        assert n_chunks % 2 == 0
        half = n_chunks // 2
        mm_fwd(0)
        mm_bwd(n_chunks - 1)

        def step(s, first):
            gate(s - 1, of_scr, gf_r, gnf, first)
            gate(n_chunks - s, ob_scr, gb_r, gnb, first)
            mm_fwd(s)
            mm_bwd(n_chunks - 1 - s)

        def first_half(s, carry):
            step(s, True)
            return carry

        def second_half(s, carry):
            step(s, False)
            return carry

        lax.fori_loop(1, half + 1, first_half, 0, unroll=4)
        lax.fori_loop(half + 1, n_chunks, second_half, 0, unroll=4)
        gate(n_chunks - 1, of_scr, gf_r, gnf, False)
        gate(0, ob_scr, gb_r, gnb, False)

    sf_scr[...] = jnp.zeros_like(sf_scr)
    sb_scr[...] = jnp.zeros_like(sb_scr)
    scan(qc_ref, kc_ref, vc_ref, gfc_ref, gbc_ref, yc_ref, n_ctx)
    scan(q_ref, k_ref, v_ref, gf_ref, gb_ref, y_ref, n_lat)


def _ret_chunk(l, lc):
    for c in (MXU_TILE, CHUNK):
        if all(n % c == 0 and (n // c == 1 or (n // c) % 2 == 0) for n in (l, lc)):
            return c
    raise ValueError(f"unsupported sequence lengths {l}, {lc}")


def _retention(z, zc, a_f, a_b, gn_f, gn_b):
    b, l, d5 = z.shape
    lc = zc.shape[1]
    d = d5 // 5
    dk = d // N_HEADS
    c = _ret_chunk(l, lc)
    assert lc <= l

    def part_specs(n_rows):
        return [pl.BlockSpec((1, n_rows, dk), lambda bi, h, p=p: (bi, 0, p * N_HEADS + h)) for p in range(5)]

    head_vec = pl.BlockSpec((1, 1, dk), lambda bi, h: (h, 0, 0))
    gn_spec = pl.BlockSpec((1, dk), lambda bi, h: (0, h))
    kern = functools.partial(_ret_kernel, n_lat=l // c, n_ctx=lc // c)
    spread = lambda a: jnp.broadcast_to(a.astype(F32)[:, None, None], (N_HEADS, 1, dk))
    return pl.pallas_call(
        kern,
        grid=(b, N_HEADS),
        in_specs=part_specs(l) + part_specs(lc) + [head_vec, head_vec, gn_spec, gn_spec],
        out_specs=[pl.BlockSpec((1, l, dk), lambda bi, h: (bi, 0, h)),
                   pl.BlockSpec((1, lc, dk), lambda bi, h: (bi, 0, h))],
        out_shape=[jax.ShapeDtypeStruct((b, l, d), BF16), jax.ShapeDtypeStruct((b, lc, d), BF16)],
        scratch_shapes=[pltpu.VMEM((l, dk), F32), pltpu.VMEM((l, dk), F32), pltpu.VMEM((l, dk), F32),
                        pltpu.VMEM((dk, dk), F32), pltpu.VMEM((dk, dk), F32),
                        pltpu.VMEM((c, c), F32), pltpu.VMEM((c, c), F32), pltpu.VMEM((4, c, dk), BF16)],
        compiler_params=_params(),
        name="retention",
    )(z, z, z, z, z, zc, zc, zc, zc, zc, spread(a_f), spread(a_b),
      gn_f.reshape(1, d).astype(F32), gn_b.reshape(1, d).astype(F32))


def kernel(x, c, ctx, c_ctx, ada_w, ada_b, norm1_g, norm2_g, conv_pw1_w, conv_pw1_b, conv_dw_w, conv_dw_b, conv_ln_g, conv_ln_b, conv_pw2_w, conv_pw2_b, ret_w_in, ret_w_out, ret_decay_f, ret_decay_b, ret_gn_f, ret_gn_b, ffn_w_in, ffn_dw_w, ffn_dw_b, ffn_w_out, final_g):
    b, l, d = x.shape
    depth = ada_w.shape[0]
    n_mixers = 2

    rows = -(-(b + 1) // 8) * 8
    cvec = jnp.concatenate([c, c_ctx[None, :], jnp.zeros((rows - b - 1, d), F32)], axis=0)
    mod = _modulation(cvec, ada_w, ada_b)
    tables = _rope_tables(l, d // N_HEADS)

    hctx = ctx
    for i in range(depth):
        last = i == depth - 1
        j = i // n_mixers
        mod_x = mod[i, :b][:, None, :]
        mod_c = jnp.broadcast_to(mod[i, b][None, None, :], (b, 1, 6 * d))
        pre_x = pre_c = None
        if i % n_mixers == 0:
            cargs = (norm1_g[i], conv_pw1_w[j].astype(BF16), conv_pw1_b[j], conv_dw_w[j], conv_dw_b[j],
                     conv_ln_g[j], conv_ln_b[j], conv_pw2_w[j].astype(BF16), conv_pw2_b[j])
            x = _conv_mixer(x, mod_x, *cargs)
            if not last:
                hctx = _conv_mixer(hctx, mod_c, *cargs)
        else:
            w_in = ret_w_in[j].astype(BF16)
            z = _ret_project(x, mod_x, norm1_g[i], w_in, tables)
            zc = _ret_project(hctx, mod_c, norm1_g[i], w_in)
            y, yc = _retention(z, zc, ret_decay_f[j], ret_decay_b[j], ret_gn_f[j], ret_gn_b[j])
            w_out = ret_w_out[j].astype(BF16)
            pre_x, pre_c = (y, w_out), (yc, w_out)
        fargs = (norm2_g[i], ffn_w_in[i].astype(BF16), ffn_dw_w[i], ffn_dw_b[i], ffn_w_out[i].astype(BF16))
        x = _conv_ffn(x, mod_x, *fargs, pre=pre_x, final_g=final_g if last else None)
        if not last:
            hctx = _conv_ffn(hctx, mod_c, *fargs, pre=pre_c)
    return x
```
